```python
import math
import jax, jax.numpy as jnp
from jax import lax
import numpy as np

D_MODEL = 2048
BATCH = 32
SEQ = 256
DEPTH = 4
DEC_BATCH = 8
DEC_SEQ = 2048
PAST_LEN = 512

GRID_W = 64
N_MIXERS = 2
N_CONV_LAYERS = (DEPTH + 1) // 2
N_ATTN_LAYERS = DEPTH // 2
D_CONV = D_MODEL
CONV_WIDTH = 3
N_HEADS = 16
QK_DIM = 64
V_DIM = 2 * QK_DIM
QK_W = N_HEADS * 2 * QK_DIM
V_W = N_HEADS * V_DIM
ROPE_AXIS_DIM = QK_DIM // 2
ROPE_BASE = 10000.0
Q_BLOCK = 128
LN_EPS = 1e-5
SUBLN_EPS = 1e-5
DEEPNORM_ALPHA = (2.0 * DEPTH) ** 0.25
DEEPNORM_BETA = (8.0 * DEPTH) ** -0.25

kernel_name = "hybrid_diffusion_conv_diffattn_step"


def layer_norm(x, g, b):
    x32 = x.astype(jnp.float32)
    mu = jnp.mean(x32, -1, keepdims=True)
    var = jnp.mean(jnp.square(x32 - mu), -1, keepdims=True)
    y = (x32 - mu) * lax.rsqrt(var + LN_EPS) * g.astype(jnp.float32) + b.astype(jnp.float32)
    return y.astype(x.dtype)


def conv3_centred(u, w):
    up = jnp.pad(u, ((0, 0), (1, 1), (0, 0)))
    return up[:, :-2] * w[0] + up[:, 1:-1] * w[1] + up[:, 2:] * w[2]


def conv_mixer(h, w_in, w_conv, w_out):
    proj = h @ w_in
    b_gate, c_gate, u, z = jnp.split(proj, 4, axis=-1)
    y = b_gate * conv3_centred(c_gate * u, w_conv) * jax.nn.silu(z)
    return y @ w_out


def axial_rope(x, n_tokens):
    rows = n_tokens // GRID_W
    r, col = jnp.meshgrid(jnp.arange(rows), jnp.arange(GRID_W), indexing='ij')
    r = r.reshape(-1).astype(jnp.float32)
    col = col.reshape(-1).astype(jnp.float32)
    half = ROPE_AXIS_DIM // 2
    inv = ROPE_BASE ** (-jnp.arange(half, dtype=jnp.float32) / half)

    def rot(xa, pos):
        ang = pos[:, None] * inv
        cos = jnp.cos(ang)[None, :, None, None, :]
        sin = jnp.sin(ang)[None, :, None, None, :]
        x1, x2 = xa[..., :half], xa[..., half:]
        return jnp.concatenate([x1 * cos - x2 * sin, x2 * cos + x1 * sin], -1)

    x32 = x.astype(jnp.float32)
    out = jnp.concatenate([rot(x32[..., :ROPE_AXIS_DIM], r), rot(x32[..., ROPE_AXIS_DIM:], col)], -1)
    return out.astype(x.dtype)


def diff_lambda(lam_params, lambda_init):
    lp = lam_params.astype(jnp.float32)
    return jnp.exp(jnp.sum(lp[0] * lp[1])) - jnp.exp(jnp.sum(lp[2] * lp[3])) + lambda_init


def diff_attend(q, k, v, lam):
    B, Lq = q.shape[0], q.shape[1]
    nb = Lq // Q_BLOCK
    qb = q.reshape(B, nb, Q_BLOCK, N_HEADS, 2, QK_DIM).swapaxes(0, 1)
    scale = QK_DIM ** -0.5
    lam32 = lam.astype(jnp.float32)

    def block(qi):
        s = jnp.einsum('bqhcd,bkhcd->bhcqk', qi, k).astype(jnp.float32) * scale
        p = jax.nn.softmax(s, axis=-1)
        pd = p[:, :, 0] - lam32 * p[:, :, 1]
        return jnp.einsum('bhqk,bkhe->bqhe', pd.astype(v.dtype), v)

    o = lax.map(block, qb)
    return o.swapaxes(0, 1).reshape(B, Lq, N_HEADS, V_DIM)


def attn_project(h, w_in):
    B, L = h.shape[0], h.shape[1]
    proj = h @ w_in
    q, k, v, z = jnp.split(proj, [QK_W, 2 * QK_W, 2 * QK_W + V_W], axis=-1)
    return (q.reshape(B, L, N_HEADS, 2, QK_DIM), k.reshape(B, L, N_HEADS, 2, QK_DIM),
            v.reshape(B, L, N_HEADS, V_DIM), z)


def attn_finish(o, z, subln_w, lambda_init, w_out):
    o32 = o.astype(jnp.float32)
    o32 = o32 * lax.rsqrt(jnp.mean(jnp.square(o32), -1, keepdims=True) + SUBLN_EPS)
    o32 = o32 * subln_w.astype(jnp.float32) * (1.0 - lambda_init)
    B, L = o.shape[0], o.shape[1]
    y = o32.astype(z.dtype).reshape(B, L, V_W) * jax.nn.silu(z)
    return y @ w_out


def setup_inputs(seed: int = 0) -> dict:
    key = jax.random.key(seed)
    ks = jax.random.split(key, 20)
    f32 = jnp.float32
    nrm = lambda k, s: jax.random.normal(k, s, f32)
    d = D_MODEL
    return {
        "x_prompt": nrm(ks[0], (BATCH, SEQ, d)),
        "x_sample": nrm(ks[1], (DEC_BATCH, DEC_SEQ, d)),
        "cache_k": nrm(ks[2], (DEC_BATCH, N_ATTN_LAYERS, PAST_LEN, N_HEADS, 2, QK_DIM)),
        "cache_v": nrm(ks[3], (DEC_BATCH, N_ATTN_LAYERS, PAST_LEN, N_HEADS, V_DIM)),
        "c": nrm(ks[4], (DEC_BATCH, d)),
        "c_ctx": nrm(ks[5], (d,)),
        "ada_w": nrm(ks[6], (DEPTH, d, 3 * d)) * (0.5 * d ** -0.5),
        "ada_b": nrm(ks[7], (DEPTH, 3 * d)) * 0.02,
        "ln_g": 1.0 + 0.02 * nrm(ks[8], (DEPTH, d)),
        "ln_b": 0.02 * nrm(ks[9], (DEPTH, d)),
        "conv_w_in": nrm(ks[10], (N_CONV_LAYERS, d, 4 * D_CONV)) * d ** -0.5,
        "conv_w": nrm(ks[11], (N_CONV_LAYERS, CONV_WIDTH, D_CONV)) * CONV_WIDTH ** -0.5,
        "conv_w_out": nrm(ks[12], (N_CONV_LAYERS, D_CONV, d)) * (DEEPNORM_BETA * D_CONV ** -0.5),
        "attn_w_in": nrm(ks[13], (N_ATTN_LAYERS, d, 2 * QK_W + 2 * V_W)) * d ** -0.5,
        "attn_lambda": 0.1 * nrm(ks[14], (N_ATTN_LAYERS, 4, QK_DIM)),
        "attn_subln_w": 1.0 + 0.02 * nrm(ks[15], (N_ATTN_LAYERS, V_DIM)),
        "attn_w_out": nrm(ks[16], (N_ATTN_LAYERS, V_W, d)) * (DEEPNORM_BETA * V_W ** -0.5),
    }


def reference(x_prompt, x_sample, cache_k, cache_v, c, c_ctx, ada_w, ada_b, ln_g, ln_b,
              conv_w_in, conv_w, conv_w_out, attn_w_in, attn_lambda, attn_subln_w, attn_w_out):
    xp, xs = x_prompt, x_sample
    n_lat = xs.shape[1]
    s_ctx = jax.nn.silu(c_ctx)
    s_lat = jax.nn.silu(c)
    new_k, new_v = [], []
    for i in range(DEPTH):
        j = i // N_MIXERS
        mod_p = s_ctx @ ada_w[i] + ada_b[i]
        mod_s = (s_lat @ ada_w[i] + ada_b[i])[:, None, :]
        sh_p, sc_p, g_p = jnp.split(mod_p, 3, axis=-1)
        sh_s, sc_s, g_s = jnp.split(mod_s, 3, axis=-1)
        hp = xp * (1.0 + sc_p) + sh_p
        hs = xs * (1.0 + sc_s) + sh_s
        if i % N_MIXERS == 0:
            op = conv_mixer(hp, conv_w_in[j], conv_w[j], conv_w_out[j])
            os_ = conv_mixer(hs, conv_w_in[j], conv_w[j], conv_w_out[j])
        else:
            lam_init = 0.8 - 0.6 * math.exp(-0.3 * i)
            lam = diff_lambda(attn_lambda[j], lam_init)
            qp, kp, vp, zp = attn_project(hp, attn_w_in[j])
            op = attn_finish(diff_attend(qp, kp, vp, lam), zp, attn_subln_w[j], lam_init, attn_w_out[j])
            new_k.append(kp)
            new_v.append(vp)
            qs, ks_, vs, zs = attn_project(hs, attn_w_in[j])
            qs = axial_rope(qs, n_lat)
            ks_ = axial_rope(ks_, n_lat)
            k_all = jnp.concatenate([ks_, cache_k[:, j].astype(ks_.dtype)], axis=1)
            v_all = jnp.concatenate([vs, cache_v[:, j].astype(vs.dtype)], axis=1)
            os_ = attn_finish(diff_attend(qs, k_all, v_all, lam), zs, attn_subln_w[j], lam_init, attn_w_out[j])
        xp = layer_norm(DEEPNORM_ALPHA * xp + g_p * op, ln_g[i], ln_b[i])
        xs = layer_norm(DEEPNORM_ALPHA * xs + g_s * os_, ln_g[i], ln_b[i])
    new_cache_k = jnp.stack(new_k, axis=1)
    new_cache_v = jnp.stack(new_v, axis=1)
    return (xp, xs, new_cache_k, new_cache_v)
```

```python
import functools
import math

import jax
import jax.numpy as jnp
from jax import lax
from jax.experimental import pallas as pl
from jax.experimental.pallas import tpu as pltpu

D_MODEL = 2048
DEPTH = 4
GRID_W = 64
N_HEADS = 16
QK_DIM = 64
HEAD_W = 2 * QK_DIM
ROPE_AXIS_DIM = QK_DIM // 2
ROPE_BASE = 10000.0
LN_EPS = 1e-5
SUBLN_EPS = 1e-5
DEEPNORM_ALPHA = (2.0 * DEPTH) ** 0.25

MOD_ROWS = 16
HALO = 16
VMEM_LIMIT_BYTES = 56 * 1024 * 1024

TM = 512
TC_CONV = 512
TC_PROJ = 512
TN_MOD = 1024
TQ = 256

F32 = jnp.float32
BF16 = jnp.bfloat16


def _cparams(sem):
    return pltpu.CompilerParams(dimension_semantics=sem, vmem_limit_bytes=VMEM_LIMIT_BYTES)


def _sigmoid(x):
    return 1.0 / (1.0 + jnp.exp(-x))


def _bdot(a, b):
    return jnp.dot(a, b, preferred_element_type=F32)


def _deepnorm_ln(x, gate, o, g, b):
    r = DEEPNORM_ALPHA * x + gate * o
    mu = jnp.mean(r, axis=-1, keepdims=True)
    rc = r - mu
    var = jnp.mean(rc * rc, axis=-1, keepdims=True)
    return rc * lax.rsqrt(var + LN_EPS) * g + b


def _mod_kernel(c_ref, w_ref, b_ref, o_ref):
    cc = c_ref[...]
    s = (cc * _sigmoid(cc)).astype(BF16)
    o_ref[...] = _bdot(s, w_ref[...].astype(BF16)) + b_ref[...]


def _modulation(cond, ada_w, ada_b):
    d3 = ada_w.shape[-1]
    return pl.pallas_call(
        _mod_kernel,
        grid=(DEPTH, d3 // TN_MOD),
        in_specs=[
            pl.BlockSpec((MOD_ROWS, D_MODEL), lambda i, n: (0, 0)),
            pl.BlockSpec((None, D_MODEL, TN_MOD), lambda i, n: (i, 0, n)),
            pl.BlockSpec((None, 1, TN_MOD), lambda i, n: (i, 0, n)),
        ],
        out_specs=pl.BlockSpec((None, MOD_ROWS, TN_MOD), lambda i, n: (i, 0, n)),
        out_shape=jax.ShapeDtypeStruct((DEPTH, MOD_ROWS, d3), F32),
        compiler_params=_cparams(("arbitrary", "arbitrary")),
        name="adaln_modulation",
    )(cond, ada_w, ada_b.reshape(DEPTH, 1, d3))


def _conv_kernel(x_ref, xp_ref, xn_ref, mod_ref, wb_ref, wc_ref, wu_ref, wz_ref, cw_ref,
                 wo_ref, g_ref, b_ref, o_ref, h_scr, *, seq_len, n_chunks):
    t = pl.program_id(0)
    k = pl.program_id(1)
    tm = x_ref.shape[0]
    d = x_ref.shape[1]

    @pl.when(k == 0)
    def _():
        m = mod_ref[...]
        shift = m[:, :d]
        scale1 = 1.0 + m[:, d:2 * d]
        h_scr[0:HALO, :] = (xp_ref[...] * scale1 + shift).astype(BF16)
        h_scr[HALO:HALO + tm, :] = (x_ref[...] * scale1 + shift).astype(BF16)
        h_scr[HALO + tm:, :] = (xn_ref[...] * scale1 + shift).astype(BF16)
        o_ref[...] = jnp.zeros_like(o_ref)

    h_all = h_scr[...]
    h_main = h_scr[HALO:HALO + tm, :]
    pb = _bdot(h_main, wb_ref[...])
    pz = _bdot(h_main, wz_ref[...])
    v = _bdot(h_all, wc_ref[...]) * _bdot(h_all, wu_ref[...])
    rows = v.shape[0]
    v_prev = pltpu.roll(v, 1, 0)[HALO:HALO + tm]
    v_next = pltpu.roll(v, rows - 1, 0)[HALO:HALO + tm]
    v_mid = v[HALO:HALO + tm]
    pos = (t * tm + lax.broadcasted_iota(jnp.int32, (tm, 1), 0)) % seq_len
    v_prev = jnp.where(pos == 0, 0.0, v_prev)
    v_next = jnp.where(pos == seq_len - 1, 0.0, v_next)
    cw = cw_ref[...]
    conv = v_prev * cw[0:1] + v_mid * cw[1:2] + v_next * cw[2:3]
    y = pb * conv * (pz * _sigmoid(pz))
    o_ref[...] += _bdot(y.astype(BF16), wo_ref[...])

    @pl.when(k == n_chunks - 1)
    def _():
        gate = mod_ref[...][:, 2 * d:]
        o_ref[...] = _deepnorm_ln(x_ref[...], gate, o_ref[...], g_ref[...], b_ref[...])


def _conv_layer(x2d, mod3, mod_row, w_in, cw8, w_out, ln_g, ln_b, seq_len):
    n_tok, d = x2d.shape
    n_chunks = d // TC_CONV
    hb = TM // HALO
    last_hb = n_tok // HALO - 1
    wspec = lambda g: pl.BlockSpec((d, TC_CONV), lambda t, k, g=g: (0, g * n_chunks + k))
    return pl.pallas_call(
        functools.partial(_conv_kernel, seq_len=seq_len, n_chunks=n_chunks),
        grid=(n_tok // TM, n_chunks),
        in_specs=[
            pl.BlockSpec((TM, d), lambda t, k: (t, 0)),
            pl.BlockSpec((HALO, d), lambda t, k: (jnp.maximum(t * hb - 1, 0), 0)),
            pl.BlockSpec((HALO, d), lambda t, k: (jnp.minimum((t + 1) * hb, last_hb), 0)),
            pl.BlockSpec((None, 1, 3 * d), lambda t, k: (mod_row(t), 0, 0)),
            wspec(0), wspec(1), wspec(2), wspec(3),
            pl.BlockSpec((8, TC_CONV), lambda t, k: (0, k)),
            pl.BlockSpec((TC_CONV, d), lambda t, k: (k, 0)),
            pl.BlockSpec((1, d), lambda t, k: (0, 0)),
            pl.BlockSpec((1, d), lambda t, k: (0, 0)),
        ],
        out_specs=pl.BlockSpec((TM, d), lambda t, k: (t, 0)),
        out_shape=jax.ShapeDtypeStruct((n_tok, d), F32),
        scratch_shapes=[pltpu.VMEM((TM + 2 * HALO, d), BF16)],
        compiler_params=_cparams(("arbitrary", "arbitrary")),
        name="conv_layer",
    )(x2d, x2d, x2d, mod3, w_in, w_in, w_in, w_in, cw8, w_out, ln_g, ln_b)


def _rope_head(seg, cos, sin_signed):
    lane = lax.broadcasted_iota(jnp.int32, seg.shape, 1)
    partner = jnp.where(lane % 32 < 16, pltpu.roll(seg, HEAD_W - 16, 1), pltpu.roll(seg, 16, 1))
    return seg * cos + partner * sin_signed


def _proj_kernel(x_ref, mod_ref, w_ref, cos_ref, sin_ref, q_ref, k_ref, v_ref, z_ref, h_scr,
                 *, rope, cols_per_group):
    j = pl.program_id(1)
    d = x_ref.shape[1]
    tc = w_ref.shape[1]

    @pl.when(j == 0)
    def _():
        m = mod_ref[...]
        h_scr[...] = (x_ref[...] * (1.0 + m[:, d:2 * d]) + m[:, :d]).astype(BF16)

    p = _bdot(h_scr[...], w_ref[...])
    group = j // cols_per_group

    def rotated(scale):
        segs = []
        for c in range(tc // HEAD_W):
            seg = p[:, c * HEAD_W:(c + 1) * HEAD_W]
            if rope:
                seg = _rope_head(seg, cos_ref[...], sin_ref[...])
            segs.append(seg if scale is None else seg * scale)
        return jnp.concatenate(segs, axis=1)

    @pl.when(group == 0)
    def _():
        q_ref[...] = rotated(QK_DIM ** -0.5).astype(q_ref.dtype)

    @pl.when(group == 1)
    def _():
        k_ref[...] = rotated(None).astype(k_ref.dtype)

    @pl.when(group == 2)
    def _():
        v_ref[...] = p.astype(v_ref.dtype)

    @pl.when(group == 3)
    def _():
        z_ref[...] = p


def _attn_project(x2d, mod3, mod_row, w_in, cos, sin_signed, seq_len, rope, kv_dtype):
    n_tok, d = x2d.shape
    cpg = d // TC_PROJ
    tiles_per_seq = seq_len // TM if seq_len >= TM else 1
    tbl_rows = TM
    tbl_map = (lambda t, j: (t % tiles_per_seq, 0)) if rope else (lambda t, j: (0, 0))

    def ospec(g):
        return pl.BlockSpec((TM, TC_PROJ), lambda t, j, g=g: (t, jnp.clip(j - g * cpg, 0, cpg - 1)))

    return pl.pallas_call(
        functools.partial(_proj_kernel, rope=rope, cols_per_group=cpg),
        grid=(n_tok // TM, 4 * cpg),
        in_specs=[
            pl.BlockSpec((TM, d), lambda t, j: (t, 0)),
            pl.BlockSpec((None, 1, 3 * d), lambda t, j: (mod_row(t), 0, 0)),
            pl.BlockSpec((d, TC_PROJ), lambda t, j: (0, j)),
            pl.BlockSpec((tbl_rows, HEAD_W), tbl_map),
            pl.BlockSpec((tbl_rows, HEAD_W), tbl_map),
        ],
        out_specs=[ospec(0), ospec(1), ospec(2), ospec(3)],
        out_shape=[
            jax.ShapeDtypeStruct((n_tok, d), BF16),
            jax.ShapeDtypeStruct((n_tok, d), kv_dtype),
            jax.ShapeDtypeStruct((n_tok, d), kv_dtype),
            jax.ShapeDtypeStruct((n_tok, d), F32),
        ],
        scratch_shapes=[pltpu.VMEM((TM, d), BF16)],
        compiler_params=_cparams(("arbitrary", "arbitrary")),
        name="attn_project",
    )(x2d, mod3, w_in, cos, sin_signed)


def _diff_lambda(lam_ref, lam_init):
    lp = lam_ref[...]
    a = jnp.sum(lp[0:1] * lp[1:2], axis=-1, keepdims=True)
    b = jnp.sum(lp[2:3] * lp[3:4], axis=-1, keepdims=True)
    return jnp.exp(a) - jnp.exp(b) + lam_init


def _softmax_maps(q, keys):
    lane = lax.broadcasted_iota(jnp.int32, q.shape, 1)
    out = []
    for c in range(2):
        in_map = (lane < QK_DIM) if c == 0 else (lane >= QK_DIM)
        qc = jnp.where(in_map, q, jnp.zeros_like(q))
        s = [lax.dot_general(qc, kb, (((1,), (1,)), ((), ())), preferred_element_type=F32)
             for kb in keys]
        m = functools.reduce(jnp.maximum, [jnp.max(sb, axis=-1, keepdims=True) for sb in s])
        e = [jnp.exp(sb - m) for sb in s]
        l = functools.reduce(jnp.add, [jnp.sum(eb, axis=-1, keepdims=True) for eb in e])
        out.append((e, l))
    return out


def _attn_body(q, keys, values, z, lam, subln, lam_init):
    (e0, l0), (e1, l1) = _softmax_maps(q, keys)
    w0 = 1.0 / l0
    w1 = lam / l1
    o = None
    for eb0, eb1, vb in zip(e0, e1, values):
        pd = (eb0 * w0 - eb1 * w1).astype(BF16)
        ob = _bdot(pd, vb)
        o = ob if o is None else o + ob
    o = o * lax.rsqrt(jnp.mean(o * o, axis=-1, keepdims=True) + SUBLN_EPS)
    o = o * subln * (1.0 - lam_init)
    return o * (z * _sigmoid(z))


def _attn_latent_kernel(q_ref, k_ref, v_ref, ck_ref, cv_ref, z_ref, lam_ref, sub_ref, y_ref,
                        *, lam_init):
    lam = _diff_lambda(lam_ref, lam_init)
    keys = [k_ref[...], ck_ref[...].astype(BF16)]
    values = [v_ref[...], cv_ref[...].astype(BF16)]
    y = _attn_body(q_ref[...], keys, values, z_ref[...], lam, sub_ref[...], lam_init)
    y_ref[...] = y.astype(y_ref.dtype)


def _attn_context_kernel(q_ref, k_ref, v_ref, z_ref, lam_ref, sub_ref, y_ref, *, lam_init):
    lam = _diff_lambda(lam_ref, lam_init)
    keys = [k_ref[...].astype(BF16)]
    values = [v_ref[...].astype(BF16)]
    y = _attn_body(q_ref[...], keys, values, z_ref[...], lam, sub_ref[...], lam_init)
    y_ref[...] = y.astype(y_ref.dtype)


def _attn_latent(q, k, v, cache_k, cache_v, layer, z, lam_p, subln, lam_init):
    b, l, hw = q.shape
    p = cache_k.shape[2]
    qspec = pl.BlockSpec((None, TQ, HEAD_W), lambda bi, h, qi: (bi, qi, h))
    kvspec = pl.BlockSpec((None, l, HEAD_W), lambda bi, h, qi: (bi, 0, h))
    cspec = pl.BlockSpec((None, None, p, HEAD_W), lambda bi, h, qi: (bi, layer, 0, h))
    return pl.pallas_call(
        functools.partial(_attn_latent_kernel, lam_init=lam_init),
        grid=(b, N_HEADS, l // TQ),
        in_specs=[qspec, kvspec, kvspec, cspec, cspec, qspec,
                  pl.BlockSpec((4, QK_DIM), lambda bi, h, qi: (0, 0)),
                  pl.BlockSpec((1, HEAD_W), lambda bi, h, qi: (0, 0))],
        out_specs=qspec,
        out_shape=jax.ShapeDtypeStruct((b, l, hw), BF16),
        compiler_params=_cparams(("arbitrary", "arbitrary", "arbitrary")),
        name="diff_attention_latent",
    )(q, k, v, cache_k, cache_v, z, lam_p, subln)


def _attn_context(q, k, v, z, lam_p, subln, lam_init):
    b, l, hw = q.shape
    spec = pl.BlockSpec((None, l, HEAD_W), lambda bi, h: (bi, 0, h))
    return pl.pallas_call(
        functools.partial(_attn_context_kernel, lam_init=lam_init),
        grid=(b, N_HEADS),
        in_specs=[spec, spec, spec, spec,
                  pl.BlockSpec((4, QK_DIM), lambda bi, h: (0, 0)),
                  pl.BlockSpec((1, HEAD_W), lambda bi, h: (0, 0))],
        out_specs=spec,
        out_shape=jax.ShapeDtypeStruct((b, l, hw), BF16),
        compiler_params=_cparams(("arbitrary", "arbitrary")),
        name="diff_attention_context",
    )(q, k, v, z, lam_p, subln)


def _finish_kernel(y_ref, x_ref, mod_ref, w_ref, g_ref, b_ref, o_ref):
    d = x_ref.shape[1]
    o = _bdot(y_ref[...], w_ref[...])
    gate = mod_ref[...][:, 2 * d:]
    o_ref[...] = _deepnorm_ln(x_ref[...], gate, o, g_ref[...], b_ref[...])


def _attn_finish(y2d, x2d, mod3, mod_row, w_out, ln_g, ln_b):
    n_tok, d = x2d.shape
    return pl.pallas_call(
        _finish_kernel,
        grid=(n_tok // TM,),
        in_specs=[
            pl.BlockSpec((TM, d), lambda t: (t, 0)),
            pl.BlockSpec((TM, d), lambda t: (t, 0)),
            pl.BlockSpec((None, 1, 3 * d), lambda t: (mod_row(t), 0, 0)),
            pl.BlockSpec((d, d), lambda t: (0, 0)),
            pl.BlockSpec((1, d), lambda t: (0, 0)),
            pl.BlockSpec((1, d), lambda t: (0, 0)),
        ],
        out_specs=pl.BlockSpec((TM, d), lambda t: (t, 0)),
        out_shape=jax.ShapeDtypeStruct((n_tok, d), F32),
        compiler_params=_cparams(("arbitrary",)),
        name="attn_finish",
    )(y2d, x2d, mod3, w_out, ln_g, ln_b)


def _rope_tables(n_tokens):
    half = ROPE_AXIS_DIM // 2
    tok = jnp.arange(n_tokens)
    row = (tok // GRID_W).astype(F32)
    col = (tok % GRID_W).astype(F32)
    inv = ROPE_BASE ** (-jnp.arange(half, dtype=F32) / half)
    lane = jnp.arange(HEAD_W)
    within = lane % ROPE_AXIS_DIM
    pos = jnp.where(((lane % QK_DIM) // ROPE_AXIS_DIM == 0)[None, :], row[:, None], col[:, None])
    ang = pos * inv[within % half][None, :]
    sign = jnp.where(within < half, -1.0, 1.0).astype(F32)
    return jnp.cos(ang), jnp.sin(ang) * sign[None, :]


def kernel(x_prompt, x_sample, cache_k, cache_v, c, c_ctx, ada_w, ada_b, ln_g, ln_b,
           conv_w_in, conv_w, conv_w_out, attn_w_in, attn_lambda, attn_subln_w, attn_w_out):
    batch, seq, d = x_prompt.shape
    dec_batch, dec_seq, _ = x_sample.shape
    n_attn, past = cache_k.shape[1], cache_k.shape[2]
    assert d == D_MODEL and dec_batch + 1 <= MOD_ROWS
    assert seq % TQ == 0 and dec_seq % TM == 0 and (batch * seq) % TM == 0 and TM % seq == 0

    cond = jnp.zeros((MOD_ROWS, d), F32).at[:dec_batch].set(c).at[dec_batch].set(c_ctx)
    mod = _modulation(cond, ada_w, ada_b)

    ctx_row = lambda t: dec_batch
    tiles_per_lat = dec_seq // TM
    lat_row = lambda t: t // tiles_per_lat

    xp = x_prompt.reshape(batch * seq, d)
    xs = x_sample.reshape(dec_batch * dec_seq, d)
    ck = cache_k.reshape(dec_batch, n_attn, past, d)
    cv = cache_v.reshape(dec_batch, n_attn, past, d)
    cos, sin_signed = _rope_tables(dec_seq)
    cw8 = jnp.zeros((conv_w.shape[0], 8, d), F32).at[:, :conv_w.shape[1]].set(conv_w)

    new_k, new_v = [], []
    for i in range(DEPTH):
        j = i // 2
        mod3 = mod[i].reshape(MOD_ROWS, 1, 3 * d)
        g = ln_g[i].reshape(1, d)
        b = ln_b[i].reshape(1, d)
        if i % 2 == 0:
            w_in = conv_w_in[j].astype(BF16)
            w_out = conv_w_out[j].astype(BF16)
            xp = _conv_layer(xp, mod3, ctx_row, w_in, cw8[j], w_out, g, b, seq)
            xs = _conv_layer(xs, mod3, lat_row, w_in, cw8[j], w_out, g, b, dec_seq)
        else:
            lam_init = 0.8 - 0.6 * math.exp(-0.3 * i)
            w_in = attn_w_in[j].astype(BF16)
            w_out = attn_w_out[j].astype(BF16)
            lam_p = attn_lambda[j]
            subln = attn_subln_w[j].reshape(1, HEAD_W)

            qp, kp, vp, zp = _attn_project(xp, mod3, ctx_row, w_in, cos, sin_signed, seq, False, F32)
            new_k.append(kp.reshape(batch, seq, N_HEADS, 2, QK_DIM))
            new_v.append(vp.reshape(batch, seq, N_HEADS, HEAD_W))
            r3 = lambda a: a.reshape(batch, seq, d)
            yp = _attn_context(r3(qp), r3(kp), r3(vp), r3(zp), lam_p, subln, lam_init)
            xp = _attn_finish(yp.reshape(batch * seq, d), xp, mod3, ctx_row, w_out, g, b)

            qs, ks, vs, zs = _attn_project(xs, mod3, lat_row, w_in, cos, sin_signed, dec_seq, True, BF16)
            s3 = lambda a: a.reshape(dec_batch, dec_seq, d)
            ys = _attn_latent(s3(qs), s3(ks), s3(vs), ck, cv, j, s3(zs), lam_p, subln, lam_init)
            xs = _attn_finish(ys.reshape(dec_batch * dec_seq, d), xs, mod3, lat_row, w_out, g, b)

    return (xp.reshape(batch, seq, d), xs.reshape(dec_batch, dec_seq, d),
            jnp.stack(new_k, axis=1), jnp.stack(new_v, axis=1))
```

```python
import functools
import math

import jax
import jax.numpy as jnp
from jax import lax
from jax.experimental import pallas as pl
from jax.experimental.pallas import tpu as pltpu

D_MODEL = 2048
DEPTH = 4
GRID_W = 64
N_HEADS = 16
QK_DIM = 64
HEAD_W = 2 * QK_DIM
ROPE_AXIS_DIM = QK_DIM // 2
ROPE_BASE = 10000.0
LN_EPS = 1e-5
SUBLN_EPS = 1e-5
DEEPNORM_ALPHA = (2.0 * DEPTH) ** 0.25
Q_SCALE = QK_DIM ** -0.5 * math.log2(math.e)

MOD_ROWS = 16
HALO = 16
VMEM_LIMIT_BYTES = 56 * 1024 * 1024

TM = 512
TC_CONV = 512
TM_PROJ = 1024
TN_PROJ = 1024
TN_MOD = 1024
TQ = 256
HEADS_PER_STEP = 4
ROW_GROUP = 16

F32 = jnp.float32
BF16 = jnp.bfloat16


def _cparams(sem):
    return pltpu.CompilerParams(dimension_semantics=sem, vmem_limit_bytes=VMEM_LIMIT_BYTES)


def _sigmoid(x):
    return 1.0 / (1.0 + jnp.exp(-x))


def _bdot(a, b):
    return jnp.dot(a, b, preferred_element_type=F32)


def _modulate(x, m):
    d = x.shape[-1]
    return (x * (1.0 + m[:, d:2 * d]) + m[:, :d]).astype(BF16)


def _deepnorm_ln(x, gate, o, g, b):
    r = DEEPNORM_ALPHA * x + gate * o
    mu = jnp.mean(r, axis=-1, keepdims=True)
    rc = r - mu
    var = jnp.mean(rc * rc, axis=-1, keepdims=True)
    return rc * lax.rsqrt(var + LN_EPS) * g + b


def _mod_kernel(c_ref, w_ref, b_ref, o_ref):
    cc = c_ref[...]
    s = (cc * _sigmoid(cc)).astype(BF16)
    o_ref[...] = _bdot(s, w_ref[...].astype(BF16)) + b_ref[...]


def _modulation(cond, ada_w, ada_b):
    d3 = ada_w.shape[-1]
    return pl.pallas_call(
        _mod_kernel,
        grid=(DEPTH, d3 // TN_MOD),
        in_specs=[
            pl.BlockSpec((MOD_ROWS, D_MODEL), lambda i, n: (0, 0)),
            pl.BlockSpec((None, D_MODEL, TN_MOD), lambda i, n: (i, 0, n)),
            pl.BlockSpec((None, 1, TN_MOD), lambda i, n: (i, 0, n)),
        ],
        out_specs=pl.BlockSpec((None, MOD_ROWS, TN_MOD), lambda i, n: (i, 0, n)),
        out_shape=jax.ShapeDtypeStruct((DEPTH, MOD_ROWS, d3), F32),
        compiler_params=_cparams(("arbitrary", "arbitrary")),
        name="adaln_modulation",
    )(cond, ada_w, ada_b.reshape(DEPTH, 1, d3))


def _first_h_kernel(x_ref, mod_ref, h_ref):
    h_ref[...] = _modulate(x_ref[...], mod_ref[...])


def _first_h(x2d, mod3, mod_row):
    n_tok, d = x2d.shape
    return pl.pallas_call(
        _first_h_kernel,
        grid=(n_tok // TM,),
        in_specs=[pl.BlockSpec((TM, d), lambda t: (t, 0)),
                  pl.BlockSpec((None, 1, 3 * d), lambda t: (mod_row(t), 0, 0))],
        out_specs=pl.BlockSpec((TM, d), lambda t: (t, 0)),
        out_shape=jax.ShapeDtypeStruct((n_tok, d), BF16),
        compiler_params=_cparams(("arbitrary",)),
        name="modulate_first",
    )(x2d, mod3)


def _conv_kernel(h_ref, hp_ref, hn_ref, x_ref, mod_ref, modn_ref, wb_ref, wc_ref, wu_ref, wz_ref,
                 cw_ref, wo_ref, g_ref, b_ref, o_ref, ho_ref, h_scr, *, seq_len, n_chunks):
    t = pl.program_id(0)
    k = pl.program_id(1)
    tm, d = x_ref.shape

    @pl.when(k == 0)
    def _():
        h_scr[0:HALO, :] = hp_ref[...]
        h_scr[HALO:HALO + tm, :] = h_ref[...]
        h_scr[HALO + tm:, :] = hn_ref[...]
        o_ref[...] = jnp.zeros_like(o_ref)

    h_all = h_scr[...]
    h_main = h_ref[...]
    pb = _bdot(h_main, wb_ref[...])
    pz = _bdot(h_main, wz_ref[...])
    v = _bdot(h_all, wc_ref[...]) * _bdot(h_all, wu_ref[...])
    rows = v.shape[0]
    v_prev = pltpu.roll(v, 1, 0)[HALO:HALO + tm]
    v_next = pltpu.roll(v, rows - 1, 0)[HALO:HALO + tm]
    v_mid = v[HALO:HALO + tm]
    pos = (t * tm + lax.broadcasted_iota(jnp.int32, (tm, 1), 0)) % seq_len
    v_prev = jnp.where(pos == 0, 0.0, v_prev)
    v_next = jnp.where(pos == seq_len - 1, 0.0, v_next)
    cw = cw_ref[...]
    conv = v_prev * cw[0:1] + v_mid * cw[1:2] + v_next * cw[2:3]
    y = pb * conv * (pz * _sigmoid(pz))
    o_ref[...] += _bdot(y.astype(BF16), wo_ref[...])

    @pl.when(k == n_chunks - 1)
    def _():
        gate = mod_ref[...][:, 2 * d:]
        xn = _deepnorm_ln(x_ref[...], gate, o_ref[...], g_ref[...], b_ref[...])
        o_ref[...] = xn
        ho_ref[...] = _modulate(xn, modn_ref[...])


def _conv_layer(h2d, x2d, mod3, modn3, mod_row, w_in, cw8, w_out, ln_g, ln_b, seq_len):
    n_tok, d = x2d.shape
    n_chunks = d // TC_CONV
    hb = TM // HALO
    last_hb = n_tok // HALO - 1
    wspec = lambda g: pl.BlockSpec((d, TC_CONV), lambda t, k, g=g: (0, g * n_chunks + k))
    tile = pl.BlockSpec((TM, d), lambda t, k: (t, 0))
    modspec = pl.BlockSpec((None, 1, 3 * d), lambda t, k: (mod_row(t), 0, 0))
    vec = pl.BlockSpec((1, d), lambda t, k: (0, 0))
    return pl.pallas_call(
        functools.partial(_conv_kernel, seq_len=seq_len, n_chunks=n_chunks),
        grid=(n_tok // TM, n_chunks),
        in_specs=[
            tile,
            pl.BlockSpec((HALO, d), lambda t, k: (jnp.maximum(t * hb - 1, 0), 0)),
            pl.BlockSpec((HALO, d), lambda t, k: (jnp.minimum((t + 1) * hb, last_hb), 0)),
            tile, modspec, modspec,
            wspec(0), wspec(1), wspec(2), wspec(3),
            pl.BlockSpec((8, TC_CONV), lambda t, k: (0, k)),
            pl.BlockSpec((TC_CONV, d), lambda t, k: (k, 0)),
            vec, vec,
        ],
        out_specs=[tile, tile],
        out_shape=[jax.ShapeDtypeStruct((n_tok, d), F32), jax.ShapeDtypeStruct((n_tok, d), BF16)],
        scratch_shapes=[pltpu.VMEM((TM + 2 * HALO, d), BF16)],
        compiler_params=_cparams(("arbitrary", "arbitrary")),
        name="conv_layer",
    )(h2d, h2d, h2d, x2d, mod3, modn3, w_in, w_in, w_in, w_in, cw8, w_out, ln_g, ln_b)


def _rope_head(seg, cos, sin_signed):
    lane = lax.broadcasted_iota(jnp.int32, seg.shape, 1)
    partner = jnp.where(lane % 32 < 16, pltpu.roll(seg, HEAD_W - 16, 1), pltpu.roll(seg, 16, 1))
    return seg * cos + partner * sin_signed


def _proj_kernel(*refs, rope, scale):
    if rope:
        h_ref, w_ref, cos_ref, sin_ref, o_ref = refs
    else:
        h_ref, w_ref, o_ref = refs
    p = _bdot(h_ref[...], w_ref[...])
    if rope:
        cos, sin = cos_ref[...], sin_ref[...]
        for c in range(p.shape[1] // HEAD_W):
            cols = slice(c * HEAD_W, (c + 1) * HEAD_W)
            o_ref[:, cols] = _rope_head(p[:, cols], cos, sin).astype(o_ref.dtype)
    elif scale is not None:
        o_ref[...] = (p * scale).astype(o_ref.dtype)
    else:
        o_ref[...] = p.astype(o_ref.dtype)


def _project(h2d, w, col0, n_cols, out_dtype, *, scale=None, rope_tables=None, seq_len=None):
    n_tok, d = h2d.shape
    rope = rope_tables is not None
    in_specs = [pl.BlockSpec((TM_PROJ, d), lambda t, j: (t, 0)),
                pl.BlockSpec((d, TN_PROJ), lambda t, j: (0, col0 // TN_PROJ + j))]
    args = [h2d, w]
    if rope:
        tiles_per_seq = seq_len // TM_PROJ
        blocks_per_group = d // TN_PROJ
        tbl = pl.BlockSpec((None, TM_PROJ, HEAD_W),
                           lambda t, j: (j // blocks_per_group, t % tiles_per_seq, 0))
        in_specs += [tbl, tbl]
        args += list(rope_tables)
    return pl.pallas_call(
        functools.partial(_proj_kernel, rope=rope, scale=scale),
        grid=(n_tok // TM_PROJ, n_cols // TN_PROJ),
        in_specs=in_specs,
        out_specs=pl.BlockSpec((TM_PROJ, TN_PROJ), lambda t, j: (t, j)),
        out_shape=jax.ShapeDtypeStruct((n_tok, n_cols), out_dtype),
        compiler_params=_cparams(("arbitrary", "arbitrary")),
        name="attn_project",
    )(*args)


def _diff_lambda(lam_ref, lam_init):
    lp = lam_ref[...]
    a = jnp.sum(lp[0:1] * lp[1:2], axis=-1, keepdims=True)
    b = jnp.sum(lp[2:3] * lp[3:4], axis=-1, keepdims=True)
    return jnp.exp(a) - jnp.exp(b) + lam_init


def _scores(q, keys):
    lane = lax.broadcasted_iota(jnp.int32, q.shape, 1)
    maps = []
    for c in range(2):
        in_map = (lane < QK_DIM) if c == 0 else (lane >= QK_DIM)
        qc = jnp.where(in_map, q, jnp.zeros_like(q))
        maps.append([lax.dot_general(qc, kb, (((1,), (1,)), ((), ())), preferred_element_type=F32)
                     for kb in keys])
    return maps


def _attend(maps, values, e_scr, z, lam, subln, lam_init):
    tq = z.shape[0]
    for c in range(2):
        s = maps[c]
        m = functools.reduce(jnp.maximum, [jnp.max(sb, axis=-1, keepdims=True) for sb in s])
        k0 = 0
        for sb in s:
            e_scr[c * tq:(c + 1) * tq, k0:k0 + sb.shape[1]] = jnp.exp2(sb - m).astype(BF16)
            k0 += sb.shape[1]
    ov, k0 = None, 0
    for vb in values:
        ob = _bdot(e_scr[:, k0:k0 + vb.shape[0]], vb)
        ov = ob if ov is None else ov + ob
        k0 += vb.shape[0]
    l0 = ov[:tq, HEAD_W:HEAD_W + 1]
    l1 = ov[tq:, HEAD_W:HEAD_W + 1]
    o = ov[:tq, :HEAD_W] * (1.0 / l0) - ov[tq:, :HEAD_W] * (lam / l1)
    o = o * lax.rsqrt(jnp.mean(o * o, axis=-1, keepdims=True) + SUBLN_EPS)
    o = o * subln * (1.0 - lam_init)
    return o * (z * _sigmoid(z))


def _attn_heads(n_heads, q_of, keys_of, values_of, z_ref, lam, subln, lam_init, y_ref, e_scr):
    maps = _scores(q_of(0), keys_of(0))
    for h in range(n_heads):
        nxt = _scores(q_of(h + 1), keys_of(h + 1)) if h + 1 < n_heads else None
        cols = slice(h * HEAD_W, (h + 1) * HEAD_W)
        y = _attend(maps, values_of(h), e_scr.at[h % 2], z_ref[:, cols], lam, subln, lam_init)
        y_ref[:, cols] = y.astype(y_ref.dtype)
        maps = nxt


def _attn_latent_kernel(q_ref, k_ref, v_ref, ck_ref, cv_ref, z_ref, lam_ref, sub_ref, y_ref,
                        e_scr, vx_scr, kc_scr, *, lam_init):
    n_heads = q_ref.shape[1] // HEAD_W
    l = k_ref.shape[0]
    cols = lambda h: slice(h * HEAD_W, (h + 1) * HEAD_W)

    @pl.when(pl.program_id(2) == 0)
    def _():
        kc_scr[...] = ck_ref[...].astype(BF16)
        for h in range(n_heads):
            vx_scr[h, 0:l, 0:HEAD_W] = v_ref[:, cols(h)]
            vx_scr[h, l:, 0:HEAD_W] = cv_ref[:, cols(h)].astype(BF16)
            vx_scr[h, :, HEAD_W:] = jnp.ones((vx_scr.shape[1], HEAD_W), BF16)

    lam = _diff_lambda(lam_ref, lam_init)
    _attn_heads(
        n_heads,
        lambda h: q_ref[:, cols(h)],
        lambda h: [k_ref[:, cols(h)], kc_scr[:, cols(h)]],
        lambda h: [vx_scr[h]],
        z_ref, lam, sub_ref[...], lam_init, y_ref, e_scr)


def _attn_context_kernel(q_ref, kv_ref, z_ref, lam_ref, sub_ref, y_ref, e_scr, vx_scr,
                         *, lam_init):
    lam = _diff_lambda(lam_ref, lam_init)
    d = q_ref.shape[1]
    n_heads = d // HEAD_W
    cols = lambda h: slice(h * HEAD_W, (h + 1) * HEAD_W)
    for h in range(n_heads):
        vx_scr[h, :, 0:HEAD_W] = kv_ref[:, d + h * HEAD_W:d + (h + 1) * HEAD_W].astype(BF16)
        vx_scr[h, :, HEAD_W:] = jnp.ones((vx_scr.shape[1], HEAD_W), BF16)
    _attn_heads(
        n_heads,
        lambda h: q_ref[:, cols(h)],
        lambda h: [kv_ref[:, cols(h)].astype(BF16)],
        lambda h: [vx_scr[h]],
        z_ref, lam, sub_ref[...], lam_init, y_ref, e_scr)


def _attn_latent(qk, v, cache_k, cache_v, layer, z, lam_p, subln, lam_init):
    b, l, hw = v.shape
    p = cache_k.shape[2]
    hb = HEADS_PER_STEP * HEAD_W
    k_off = hw // hb
    qspec = pl.BlockSpec((None, TQ, hb), lambda bi, h, qi: (bi, qi, h))
    kspec = pl.BlockSpec((None, l, hb), lambda bi, h, qi: (bi, 0, k_off + h))
    vspec = pl.BlockSpec((None, l, hb), lambda bi, h, qi: (bi, 0, h))
    cspec = pl.BlockSpec((None, None, p, hb), lambda bi, h, qi: (bi, layer, 0, h))
    return pl.pallas_call(
        functools.partial(_attn_latent_kernel, lam_init=lam_init),
        grid=(b, hw // hb, l // TQ),
        in_specs=[qspec, kspec, vspec, cspec, cspec, qspec,
                  pl.BlockSpec((4, QK_DIM), lambda bi, h, qi: (0, 0)),
                  pl.BlockSpec((1, HEAD_W), lambda bi, h, qi: (0, 0))],
        out_specs=qspec,
        out_shape=jax.ShapeDtypeStruct((b, l, hw), BF16),
        scratch_shapes=[pltpu.VMEM((2, 2 * TQ, l + p), BF16),
                        pltpu.VMEM((HEADS_PER_STEP, l + p, 2 * HEAD_W), BF16),
                        pltpu.VMEM((p, hb), BF16)],
        compiler_params=_cparams(("arbitrary", "arbitrary", "arbitrary")),
        name="diff_attention_latent",
    )(qk, qk, v, cache_k, cache_v, z, lam_p, subln)


def _attn_context(q, kv, z, lam_p, subln, lam_init):
    b, l, hw = q.shape
    spec = pl.BlockSpec((None, l, hw), lambda bi: (bi, 0, 0))
    return pl.pallas_call(
        functools.partial(_attn_context_kernel, lam_init=lam_init),
        grid=(b,),
        in_specs=[spec, pl.BlockSpec((None, l, 2 * hw), lambda bi: (bi, 0, 0)), spec,
                  pl.BlockSpec((4, QK_DIM), lambda bi: (0, 0)),
                  pl.BlockSpec((1, HEAD_W), lambda bi: (0, 0))],
        out_specs=spec,
        out_shape=jax.ShapeDtypeStruct((b, l, hw), BF16),
        scratch_shapes=[pltpu.VMEM((2, 2 * l, l), BF16),
                        pltpu.VMEM((hw // HEAD_W, l, 2 * HEAD_W), BF16)],
        compiler_params=_cparams(("arbitrary",)),
        name="diff_attention_context",
    )(q, kv, z, lam_p, subln)


def _finish_kernel(y_ref, x_ref, mod_ref, modn_ref, w_ref, g_ref, b_ref, o_ref, *ho_ref):
    d = x_ref.shape[1]
    o = _bdot(y_ref[...], w_ref[...])
    gate = mod_ref[...][:, 2 * d:]
    xn = _deepnorm_ln(x_ref[...], gate, o, g_ref[...], b_ref[...])
    o_ref[...] = xn
    if ho_ref:
        ho_ref[0][...] = _modulate(xn, modn_ref[...])


def _attn_finish(y2d, x2d, mod3, modn3, mod_row, w_out, ln_g, ln_b, emit_h):
    n_tok, d = x2d.shape
    tile = pl.BlockSpec((TM, d), lambda t: (t, 0))
    modspec = pl.BlockSpec((None, 1, 3 * d), lambda t: (mod_row(t), 0, 0))
    vec = pl.BlockSpec((1, d), lambda t: (0, 0))
    out_specs = [tile, tile] if emit_h else [tile]
    out_shape = [jax.ShapeDtypeStruct((n_tok, d), F32), jax.ShapeDtypeStruct((n_tok, d), BF16)]
    return pl.pallas_call(
        _finish_kernel,
        grid=(n_tok // TM,),
        in_specs=[tile, tile, modspec, modspec, pl.BlockSpec((d, d), lambda t: (0, 0)), vec, vec],
        out_specs=out_specs,
        out_shape=out_shape[:len(out_specs)],
        compiler_params=_cparams(("arbitrary",)),
        name="attn_finish",
    )(y2d, x2d, mod3, modn3, w_out, ln_g, ln_b)


def _rope_tables(n_tokens):
    half = ROPE_AXIS_DIM // 2
    tok = jnp.arange(n_tokens)
    row = (tok // GRID_W).astype(F32)
    col = (tok % GRID_W).astype(F32)
    inv = ROPE_BASE ** (-jnp.arange(half, dtype=F32) / half)
    lane = jnp.arange(HEAD_W)
    within = lane % ROPE_AXIS_DIM
    pos = jnp.where(((lane % QK_DIM) // ROPE_AXIS_DIM == 0)[None, :], row[:, None], col[:, None])
    ang = pos * inv[within % half][None, :]
    sign = jnp.where(within < half, -1.0, 1.0).astype(F32)
    return jnp.cos(ang), jnp.sin(ang) * sign[None, :]


def kernel(x_prompt, x_sample, cache_k, cache_v, c, c_ctx, ada_w, ada_b, ln_g, ln_b,
           conv_w_in, conv_w, conv_w_out, attn_w_in, attn_lambda, attn_subln_w, attn_w_out):
    batch, seq, d = x_prompt.shape
    dec_batch, dec_seq, _ = x_sample.shape
    n_attn, past = cache_k.shape[1], cache_k.shape[2]
    assert d == D_MODEL and dec_batch + 1 <= MOD_ROWS
    assert dec_seq % TM_PROJ == 0 and (batch * seq) % TM_PROJ == 0 and TM % seq == 0
    assert dec_seq % TQ == 0

    cond = jnp.zeros((MOD_ROWS, d), F32).at[:dec_batch].set(c).at[dec_batch].set(c_ctx)
    mod = _modulation(cond, ada_w, ada_b).reshape(DEPTH, MOD_ROWS, 1, 3 * d)

    ctx_row = lambda t: dec_batch
    tiles_per_lat = dec_seq // TM
    lat_row = lambda t: t // tiles_per_lat

    xp = x_prompt.reshape(batch * seq, d)
    xs = x_sample.reshape(dec_batch * dec_seq, d)
    ck = cache_k.reshape(dec_batch, n_attn, past, d)
    cv = cache_v.reshape(dec_batch, n_attn, past, d)
    cos, sin_signed = _rope_tables(dec_seq)
    qk_tables = (jnp.stack([cos * Q_SCALE, cos]), jnp.stack([sin_signed * Q_SCALE, sin_signed]))
    cw8 = jnp.zeros((conv_w.shape[0], 8, d), F32).at[:, :conv_w.shape[1]].set(conv_w)

    hp = _first_h(xp, mod[0], ctx_row)
    hs = _first_h(xs, mod[0], lat_row)

    new_k, new_v = [], []
    for i in range(DEPTH):
        j = i // 2
        last = i == DEPTH - 1
        mod3 = mod[i]
        modn3 = mod[min(i + 1, DEPTH - 1)]
        g = ln_g[i].reshape(1, d)
        b = ln_b[i].reshape(1, d)
        if i % 2 == 0:
            w_in = conv_w_in[j].astype(BF16)
            w_out = conv_w_out[j].astype(BF16)
            xp, hp = _conv_layer(hp, xp, mod3, modn3, ctx_row, w_in, cw8[j], w_out, g, b, seq)
            xs, hs = _conv_layer(hs, xs, mod3, modn3, lat_row, w_in, cw8[j], w_out, g, b, dec_seq)
        else:
            lam_init = 0.8 - 0.6 * math.exp(-0.3 * i)
            w_in = attn_w_in[j].astype(BF16)
            w_out = attn_w_out[j].astype(BF16)
            lam_p = attn_lambda[j]
            subln = attn_subln_w[j].reshape(1, HEAD_W)

            qp = _project(hp, w_in, 0, d, BF16, scale=Q_SCALE)
            kvp = _project(hp, w_in, d, 2 * d, F32)
            zp = _project(hp, w_in, 3 * d, d, F32)
            new_k.append(kvp[:, :d].reshape(batch, seq, N_HEADS, 2, QK_DIM))
            new_v.append(kvp[:, d:].reshape(batch, seq, N_HEADS, HEAD_W))
            yp = _attn_context(qp.reshape(batch, seq, d), kvp.reshape(batch, seq, 2 * d),
                               zp.reshape(batch, seq, d), lam_p, subln, lam_init)
            outs = _attn_finish(yp.reshape(batch * seq, d), xp, mod3, modn3, ctx_row, w_out, g, b,
                                not last)
            xp, hp = outs if not last else (outs[0], None)

            qks = _project(hs, w_in, 0, 2 * d, BF16, rope_tables=qk_tables, seq_len=dec_seq)
            vs = _project(hs, w_in, 2 * d, d, BF16)
            zs = _project(hs, w_in, 3 * d, d, F32)
            ys = _attn_latent(qks.reshape(dec_batch, dec_seq, 2 * d), vs.reshape(dec_batch, dec_seq, d),
                              ck, cv, j, zs.reshape(dec_batch, dec_seq, d), lam_p, subln, lam_init)
            outs = _attn_finish(ys.reshape(dec_batch * dec_seq, d), xs, mod3, modn3, lat_row, w_out,
                                g, b, not last)
            xs, hs = outs if not last else (outs[0], None)

    return (xp.reshape(batch, seq, d), xs.reshape(dec_batch, dec_seq, d),
            jnp.stack(new_k, axis=1), jnp.stack(new_v, axis=1))
```

```python
import functools
import math

import jax
import jax.numpy as jnp
from jax import lax
from jax.experimental import pallas as pl
from jax.experimental.pallas import tpu as pltpu

D_MODEL = 2048
DEPTH = 4
GRID_W = 64
N_HEADS = 16
QK_DIM = 64
HEAD_W = 2 * QK_DIM
ROPE_AXIS_DIM = QK_DIM // 2
ROPE_BASE = 10000.0
LN_EPS = 1e-5
SUBLN_EPS = 1e-5
DEEPNORM_ALPHA = (2.0 * DEPTH) ** 0.25
Q_SCALE = QK_DIM ** -0.5 * math.log2(math.e)

MOD_ROWS = 16
HALO = 16
VMEM_LIMIT_BYTES = 56 * 1024 * 1024

TM = 512
TC_CONV = 512
TM_PROJ = 1024
TN_PROJ = 1024
TN_MOD = 1024
TQ = 256
HEADS_PER_STEP = 4

F32 = jnp.float32
BF16 = jnp.bfloat16


def _cparams(sem):
    return pltpu.CompilerParams(dimension_semantics=sem, vmem_limit_bytes=VMEM_LIMIT_BYTES)


def _sigmoid(x):
    return 1.0 / (1.0 + jnp.exp(-x))


def _bdot(a, b):
    return jnp.dot(a, b, preferred_element_type=F32)


def _modulate(x, m):
    d = x.shape[-1]
    return (x * (1.0 + m[:, d:2 * d]) + m[:, :d]).astype(BF16)


def _deepnorm_ln(x, gate, o, g, b):
    r = DEEPNORM_ALPHA * x + gate * o
    mu = jnp.mean(r, axis=-1, keepdims=True)
    rc = r - mu
    var = jnp.mean(rc * rc, axis=-1, keepdims=True)
    return rc * lax.rsqrt(var + LN_EPS) * g + b


def _mod_spec(d, layer, row_of):
    return pl.BlockSpec((None, None, 1, 3 * d), lambda *idx: (layer, row_of(idx[0]), 0, 0))


def _vec_spec(d, layer):
    return pl.BlockSpec((None, 1, d), lambda *idx: (layer, 0, 0))


def _mod_kernel(c_ref, w_ref, b_ref, o_ref):
    cc = c_ref[...]
    s = (cc * _sigmoid(cc)).astype(BF16)
    o_ref[...] = _bdot(s, w_ref[...].astype(BF16)) + b_ref[...]


def _modulation(cond, ada_w, ada_b):
    d3 = ada_w.shape[-1]
    return pl.pallas_call(
        _mod_kernel,
        grid=(DEPTH, d3 // TN_MOD),
        in_specs=[
            pl.BlockSpec((MOD_ROWS, D_MODEL), lambda i, n: (0, 0)),
            pl.BlockSpec((None, D_MODEL, TN_MOD), lambda i, n: (i, 0, n)),
            pl.BlockSpec((None, 1, TN_MOD), lambda i, n: (i, 0, n)),
        ],
        out_specs=pl.BlockSpec((None, MOD_ROWS, TN_MOD), lambda i, n: (i, 0, n)),
        out_shape=jax.ShapeDtypeStruct((DEPTH, MOD_ROWS, d3), F32),
        compiler_params=_cparams(("arbitrary", "arbitrary")),
        name="adaln_modulation",
    )(cond, ada_w, ada_b.reshape(DEPTH, 1, d3))


def _first_h_kernel(x_ref, mod_ref, h_ref):
    h_ref[...] = _modulate(x_ref[...], mod_ref[...])


def _first_h(x2d, mod, mod_row):
    n_tok, d = x2d.shape
    return pl.pallas_call(
        _first_h_kernel,
        grid=(n_tok // TM,),
        in_specs=[pl.BlockSpec((TM, d), lambda t: (t, 0)), _mod_spec(d, 0, mod_row)],
        out_specs=pl.BlockSpec((TM, d), lambda t: (t, 0)),
        out_shape=jax.ShapeDtypeStruct((n_tok, d), BF16),
        compiler_params=_cparams(("arbitrary",)),
        name="modulate_first",
    )(x2d, mod)


def _conv_kernel(h_ref, hp_ref, hn_ref, x_ref, mod_ref, modn_ref, wb_ref, wc_ref, wu_ref, wz_ref,
                 cw_ref, wo_ref, g_ref, b_ref, o_ref, ho_ref, h_scr, *, seq_len, n_chunks):
    t = pl.program_id(0)
    k = pl.program_id(1)
    tm, d = x_ref.shape

    @pl.when(k == 0)
    def _():
        h_scr[0:HALO, :] = hp_ref[...]
        h_scr[HALO:HALO + tm, :] = h_ref[...]
        h_scr[HALO + tm:, :] = hn_ref[...]
        o_ref[...] = jnp.zeros_like(o_ref)

    h_all = h_scr[...]
    h_main = h_ref[...]
    pb = _bdot(h_main, wb_ref[...])
    pz = _bdot(h_main, wz_ref[...])
    v = _bdot(h_all, wc_ref[...]) * _bdot(h_all, wu_ref[...])
    rows = v.shape[0]
    v_prev = pltpu.roll(v, 1, 0)[HALO:HALO + tm]
    v_next = pltpu.roll(v, rows - 1, 0)[HALO:HALO + tm]
    v_mid = v[HALO:HALO + tm]
    pos = (t * tm + lax.broadcasted_iota(jnp.int32, (tm, 1), 0)) % seq_len
    v_prev = jnp.where(pos == 0, 0.0, v_prev)
    v_next = jnp.where(pos == seq_len - 1, 0.0, v_next)
    cw = cw_ref[...]
    conv = v_prev * cw[0:1] + v_mid * cw[1:2] + v_next * cw[2:3]
    y = pb * conv * (pz * _sigmoid(pz))
    o_ref[...] += _bdot(y.astype(BF16), wo_ref[...])

    @pl.when(k == n_chunks - 1)
    def _():
        gate = mod_ref[...][:, 2 * d:]
        xn = _deepnorm_ln(x_ref[...], gate, o_ref[...], g_ref[...], b_ref[...])
        o_ref[...] = xn
        ho_ref[...] = _modulate(xn, modn_ref[...])


def _conv_layer(h2d, x2d, mod, layer, mod_row, w_in, cw8, w_out, ln_g, ln_b, j, seq_len):
    n_tok, d = x2d.shape
    n_chunks = d // TC_CONV
    hb = TM // HALO
    last_hb = n_tok // HALO - 1
    wspec = lambda g: pl.BlockSpec((None, d, TC_CONV), lambda t, k, g=g: (j, 0, g * n_chunks + k))
    tile = pl.BlockSpec((TM, d), lambda t, k: (t, 0))
    return pl.pallas_call(
        functools.partial(_conv_kernel, seq_len=seq_len, n_chunks=n_chunks),
        grid=(n_tok // TM, n_chunks),
        in_specs=[
            tile,
            pl.BlockSpec((HALO, d), lambda t, k: (jnp.maximum(t * hb - 1, 0), 0)),
            pl.BlockSpec((HALO, d), lambda t, k: (jnp.minimum((t + 1) * hb, last_hb), 0)),
            tile, _mod_spec(d, layer, mod_row), _mod_spec(d, layer + 1, mod_row),
            wspec(0), wspec(1), wspec(2), wspec(3),
            pl.BlockSpec((None, 8, TC_CONV), lambda t, k: (j, 0, k)),
            pl.BlockSpec((None, TC_CONV, d), lambda t, k: (j, k, 0)),
            _vec_spec(d, layer), _vec_spec(d, layer),
        ],
        out_specs=[tile, tile],
        out_shape=[jax.ShapeDtypeStruct((n_tok, d), F32), jax.ShapeDtypeStruct((n_tok, d), BF16)],
        scratch_shapes=[pltpu.VMEM((TM + 2 * HALO, d), BF16)],
        compiler_params=_cparams(("arbitrary", "arbitrary")),
        name="conv_layer",
    )(h2d, h2d, h2d, x2d, mod, mod, w_in, w_in, w_in, w_in, cw8, w_out, ln_g, ln_b)


def _rope_head(seg, cos, sin_signed):
    lane = lax.broadcasted_iota(jnp.int32, seg.shape, 1)
    partner = jnp.where(lane % 32 < 16, pltpu.roll(seg, HEAD_W - 16, 1), pltpu.roll(seg, 16, 1))
    return seg * cos + partner * sin_signed


def _proj_kernel(*refs, rope, scale):
    h_ref, w_ref, o_ref = refs[0], refs[1], refs[-1]
    p = _bdot(h_ref[...], w_ref[...])
    if rope:
        cos, sin = refs[2][...], refs[3][...]
        for c in range(p.shape[1] // HEAD_W):
            cols = slice(c * HEAD_W, (c + 1) * HEAD_W)
            o_ref[:, cols] = _rope_head(p[:, cols], cos, sin).astype(o_ref.dtype)
    elif scale is not None:
        o_ref[...] = (p * scale).astype(o_ref.dtype)
    else:
        o_ref[...] = p.astype(o_ref.dtype).reshape(o_ref.shape)


def _project(h2d, w, j, col0, n_cols, out_dtype, *, scale=None, rope_tables=None, seq_len=None,
             cache=None):
    n_tok, d = h2d.shape
    rope = rope_tables is not None
    in_specs = [pl.BlockSpec((TM_PROJ, d), lambda t, n: (t, 0)),
                pl.BlockSpec((None, d, TN_PROJ), lambda t, n: (j, 0, col0 // TN_PROJ + n))]
    args = [h2d, w]
    aliases = {}
    if rope:
        tiles_per_seq = seq_len // TM_PROJ
        blocks_per_group = d // TN_PROJ
        tbl = pl.BlockSpec((None, TM_PROJ, HEAD_W),
                           lambda t, n: (n // blocks_per_group, t % tiles_per_seq, 0))
        in_specs += [tbl, tbl]
        args += list(rope_tables)
    if cache is None:
        out_spec = pl.BlockSpec((TM_PROJ, TN_PROJ), lambda t, n: (t, n))
        out_shape = jax.ShapeDtypeStruct((n_tok, n_cols), out_dtype)
    else:
        previous, n_layers, batch, seq = cache
        out_spec = pl.BlockSpec((TM_PROJ // seq, None, seq, TN_PROJ), lambda t, n: (t, j, 0, n))
        out_shape = jax.ShapeDtypeStruct((batch, n_layers, seq, n_cols), out_dtype)
        if previous is not None:
            in_specs.append(pl.BlockSpec(memory_space=pl.ANY))
            aliases = {len(args): 0}
            args.append(previous)
    return pl.pallas_call(
        functools.partial(_proj_kernel, rope=rope, scale=scale),
        grid=(n_tok // TM_PROJ, n_cols // TN_PROJ),
        in_specs=in_specs,
        out_specs=out_spec,
        out_shape=out_shape,
        input_output_aliases=aliases,
        compiler_params=_cparams(("arbitrary", "arbitrary")),
        name="attn_project",
    )(*args)


def _diff_lambda(lam_ref, lam_init):
    lp = lam_ref[...]
    a = jnp.sum(lp[0:1] * lp[1:2], axis=-1, keepdims=True)
    b = jnp.sum(lp[2:3] * lp[3:4], axis=-1, keepdims=True)
    return jnp.exp(a) - jnp.exp(b) + lam_init


def _scores(q, keys):
    lane = lax.broadcasted_iota(jnp.int32, q.shape, 1)
    maps = []
    for c in range(2):
        in_map = (lane < QK_DIM) if c == 0 else (lane >= QK_DIM)
        qc = jnp.where(in_map, q, jnp.zeros_like(q))
        maps.append([lax.dot_general(qc, kb, (((1,), (1,)), ((), ())), preferred_element_type=F32)
                     for kb in keys])
    return maps


def _exp_stack(maps, e_ref):
    for c in range(2):
        s = maps[c]
        tq = s[0].shape[0]
        m = functools.reduce(jnp.maximum, [jnp.max(sb, axis=-1, keepdims=True) for sb in s])
        k0 = 0
        for sb in s:
            e_ref[c * tq:(c + 1) * tq, k0:k0 + sb.shape[1]] = jnp.exp2(sb - m).astype(BF16)
            k0 += sb.shape[1]


def _combine(e_ref, vx, z, lam, subln, lam_init):
    tq = z.shape[0]
    ov = _bdot(e_ref[...], vx)
    l0 = ov[:tq, HEAD_W:HEAD_W + 1]
    l1 = ov[tq:, HEAD_W:HEAD_W + 1]
    o = ov[:tq, :HEAD_W] * (1.0 / l0) - ov[tq:, :HEAD_W] * (lam / l1)
    o = o * lax.rsqrt(jnp.mean(o * o, axis=-1, keepdims=True) + SUBLN_EPS)
    o = o * subln * (1.0 - lam_init)
    return o * (z * _sigmoid(z))


def _attn_latent_kernel(q_ref, k_ref, v_ref, ck_ref, cv_ref, z_ref, lam_ref, sub_ref, y_ref,
                        e_scr, vx_scr, kc_scr, *, lam_init):
    n_heads = q_ref.shape[1] // HEAD_W
    l = k_ref.shape[0]
    cols = lambda h: slice(h * HEAD_W, (h + 1) * HEAD_W)

    @pl.when(pl.program_id(2) == 0)
    def _():
        kc_scr[...] = ck_ref[...].astype(BF16)
        for h in range(n_heads):
            vx_scr[h, 0:l, 0:HEAD_W] = v_ref[:, cols(h)]
            vx_scr[h, l:, 0:HEAD_W] = cv_ref[:, cols(h)].astype(BF16)
            vx_scr[h, :, HEAD_W:] = jnp.ones((vx_scr.shape[1], HEAD_W), BF16)

    lam = _diff_lambda(lam_ref, lam_init)
    subln = sub_ref[...]
    scores = lambda h: _scores(q_ref[:, cols(h)], [k_ref[:, cols(h)], kc_scr[:, cols(h)]])
    maps = scores(0)
    for h in range(n_heads):
        nxt = scores(h + 1) if h + 1 < n_heads else None
        e_ref = e_scr.at[h % 2]
        _exp_stack(maps, e_ref)
        y = _combine(e_ref, vx_scr[h], z_ref[:, cols(h)], lam, subln, lam_init)
        y_ref[:, cols(h)] = y.astype(y_ref.dtype)
        maps = nxt


def _attn_context_kernel(q_ref, k_ref, v_ref, z_ref, lam_ref, sub_ref, y_ref, e_scr, vx_scr,
                         *, lam_init):
    lam = _diff_lambda(lam_ref, lam_init)
    subln = sub_ref[...]
    heads = range(q_ref.shape[1] // HEAD_W)
    cols = lambda h: slice(h * HEAD_W, (h + 1) * HEAD_W)
    for h in heads:
        vx_scr[h, :, 0:HEAD_W] = v_ref[:, cols(h)].astype(BF16)
        vx_scr[h, :, HEAD_W:] = jnp.ones((vx_scr.shape[1], HEAD_W), BF16)
    maps = [_scores(q_ref[:, cols(h)], [k_ref[:, cols(h)].astype(BF16)]) for h in heads]
    for h in heads:
        _exp_stack(maps[h], e_scr.at[h])
    for h in heads:
        y = _combine(e_scr.at[h], vx_scr[h], z_ref[:, cols(h)], lam, subln, lam_init)
        y_ref[:, cols(h)] = y.astype(y_ref.dtype)


def _attn_latent(qk, v, cache_k, cache_v, j, z, lam_p, subln, lam_init):
    b, l, hw = v.shape
    p = cache_k.shape[2]
    hb = HEADS_PER_STEP * HEAD_W
    k_off = hw // hb
    qspec = pl.BlockSpec((None, TQ, hb), lambda bi, h, qi: (bi, qi, h))
    kspec = pl.BlockSpec((None, l, hb), lambda bi, h, qi: (bi, 0, k_off + h))
    vspec = pl.BlockSpec((None, l, hb), lambda bi, h, qi: (bi, 0, h))
    cspec = pl.BlockSpec((None, None, p, hb), lambda bi, h, qi: (bi, j, 0, h))
    return pl.pallas_call(
        functools.partial(_attn_latent_kernel, lam_init=lam_init),
        grid=(b, hw // hb, l // TQ),
        in_specs=[qspec, kspec, vspec, cspec, cspec, qspec,
                  pl.BlockSpec((None, 4, QK_DIM), lambda bi, h, qi: (j, 0, 0)),
                  pl.BlockSpec((None, 1, HEAD_W), lambda bi, h, qi: (j, 0, 0))],
        out_specs=qspec,
        out_shape=jax.ShapeDtypeStruct((b, l, hw), BF16),
        scratch_shapes=[pltpu.VMEM((2, 2 * TQ, l + p), BF16),
                        pltpu.VMEM((HEADS_PER_STEP, l + p, 2 * HEAD_W), BF16),
                        pltpu.VMEM((p, hb), BF16)],
        compiler_params=_cparams(("arbitrary", "arbitrary", "arbitrary")),
        name="diff_attention_latent",
    )(qk, qk, v, cache_k, cache_v, z, lam_p, subln)


def _attn_context(q, k_new, v_new, j, z, lam_p, subln, lam_init):
    b, l, hw = q.shape
    spec = pl.BlockSpec((None, l, hw), lambda bi: (bi, 0, 0))
    kvspec = pl.BlockSpec((None, None, l, hw), lambda bi: (bi, j, 0, 0))
    n_heads = hw // HEAD_W
    return pl.pallas_call(
        functools.partial(_attn_context_kernel, lam_init=lam_init),
        grid=(b,),
        in_specs=[spec, kvspec, kvspec, spec,
                  pl.BlockSpec((None, 4, QK_DIM), lambda bi: (j, 0, 0)),
                  pl.BlockSpec((None, 1, HEAD_W), lambda bi: (j, 0, 0))],
        out_specs=spec,
        out_shape=jax.ShapeDtypeStruct((b, l, hw), BF16),
        scratch_shapes=[pltpu.VMEM((n_heads, 2 * l, l), BF16),
                        pltpu.VMEM((n_heads, l, 2 * HEAD_W), BF16)],
        compiler_params=_cparams(("arbitrary",)),
        name="diff_attention_context",
    )(q, k_new, v_new, z, lam_p, subln)


def _finish_kernel(y_ref, x_ref, mod_ref, modn_ref, w_ref, g_ref, b_ref, o_ref, *ho_ref):
    d = x_ref.shape[1]
    o = _bdot(y_ref[...], w_ref[...])
    gate = mod_ref[...][:, 2 * d:]
    xn = _deepnorm_ln(x_ref[...], gate, o, g_ref[...], b_ref[...])
    o_ref[...] = xn
    if ho_ref:
        ho_ref[0][...] = _modulate(xn, modn_ref[...])


def _attn_finish(y2d, x2d, mod, layer, mod_row, w_out, ln_g, ln_b, j, emit_h):
    n_tok, d = x2d.shape
    tile = pl.BlockSpec((TM, d), lambda t: (t, 0))
    out_specs = [tile, tile] if emit_h else [tile]
    out_shape = [jax.ShapeDtypeStruct((n_tok, d), F32), jax.ShapeDtypeStruct((n_tok, d), BF16)]
    return pl.pallas_call(
        _finish_kernel,
        grid=(n_tok // TM,),
        in_specs=[tile, tile, _mod_spec(d, layer, mod_row),
                  _mod_spec(d, min(layer + 1, DEPTH - 1), mod_row),
                  pl.BlockSpec((None, d, d), lambda t: (j, 0, 0)),
                  _vec_spec(d, layer), _vec_spec(d, layer)],
        out_specs=out_specs,
        out_shape=out_shape[:len(out_specs)],
        compiler_params=_cparams(("arbitrary",)),
        name="attn_finish",
    )(y2d, x2d, mod, mod, w_out, ln_g, ln_b)


def _rope_tables(n_tokens):
    half = ROPE_AXIS_DIM // 2
    tok = jnp.arange(n_tokens)
    row = (tok // GRID_W).astype(F32)
    col = (tok % GRID_W).astype(F32)
    inv = ROPE_BASE ** (-jnp.arange(half, dtype=F32) / half)
    lane = jnp.arange(HEAD_W)
    within = lane % ROPE_AXIS_DIM
    pos = jnp.where(((lane % QK_DIM) // ROPE_AXIS_DIM == 0)[None, :], row[:, None], col[:, None])
    ang = pos * inv[within % half][None, :]
    sign = jnp.where(within < half, -1.0, 1.0).astype(F32)
    return jnp.cos(ang), jnp.sin(ang) * sign[None, :]


def kernel(x_prompt, x_sample, cache_k, cache_v, c, c_ctx, ada_w, ada_b, ln_g, ln_b,
           conv_w_in, conv_w, conv_w_out, attn_w_in, attn_lambda, attn_subln_w, attn_w_out):
    batch, seq, d = x_prompt.shape
    dec_batch, dec_seq, _ = x_sample.shape
    n_attn, past = cache_k.shape[1], cache_k.shape[2]
    assert d == D_MODEL and dec_batch + 1 <= MOD_ROWS
    assert dec_seq % TM_PROJ == 0 and (batch * seq) % TM_PROJ == 0 and TM % seq == 0
    assert dec_seq % TQ == 0

    cond = jnp.zeros((MOD_ROWS, d), F32).at[:dec_batch].set(c).at[dec_batch].set(c_ctx)
    mod = _modulation(cond, ada_w, ada_b).reshape(DEPTH, MOD_ROWS, 1, 3 * d)

    ctx_row = lambda t: dec_batch
    tiles_per_lat = dec_seq // TM
    lat_row = lambda t: t // tiles_per_lat

    xp = x_prompt.reshape(batch * seq, d)
    xs = x_sample.reshape(dec_batch * dec_seq, d)
    ck = cache_k.reshape(dec_batch, n_attn, past, d)
    cv = cache_v.reshape(dec_batch, n_attn, past, d)
    cos, sin_signed = _rope_tables(dec_seq)
    qk_tables = (jnp.stack([cos * Q_SCALE, cos]), jnp.stack([sin_signed * Q_SCALE, sin_signed]))
    cw8 = jnp.zeros((conv_w.shape[0], 8, d), F32).at[:, :conv_w.shape[1]].set(conv_w)
    g3 = ln_g.reshape(DEPTH, 1, d)
    b3 = ln_b.reshape(DEPTH, 1, d)
    subln = attn_subln_w.reshape(n_attn, 1, HEAD_W)
    conv_in, conv_out = conv_w_in.astype(BF16), conv_w_out.astype(BF16)
    attn_in, attn_out = attn_w_in.astype(BF16), attn_w_out.astype(BF16)

    hp = _first_h(xp, mod, ctx_row)
    hs = _first_h(xs, mod, lat_row)

    k_new = v_new = None
    for i in range(DEPTH):
        j = i // 2
        last = i == DEPTH - 1
        if i % 2 == 0:
            xp, hp = _conv_layer(hp, xp, mod, i, ctx_row, conv_in, cw8, conv_out, g3, b3, j, seq)
            xs, hs = _conv_layer(hs, xs, mod, i, lat_row, conv_in, cw8, conv_out, g3, b3, j, dec_seq)
        else:
            lam_init = 0.8 - 0.6 * math.exp(-0.3 * i)

            qp = _project(hp, attn_in, j, 0, d, BF16, scale=Q_SCALE)
            k_new = _project(hp, attn_in, j, d, d, F32, cache=(k_new, n_attn, batch, seq))
            v_new = _project(hp, attn_in, j, 2 * d, d, F32, cache=(v_new, n_attn, batch, seq))
            zp = _project(hp, attn_in, j, 3 * d, d, F32)
            yp = _attn_context(qp.reshape(batch, seq, d), k_new, v_new, j, zp.reshape(batch, seq, d),
                               attn_lambda, subln, lam_init)
            outs = _attn_finish(yp.reshape(batch * seq, d), xp, mod, i, ctx_row, attn_out, g3, b3, j,
                                not last)
            xp, hp = outs if not last else (outs[0], None)

            qks = _project(hs, attn_in, j, 0, 2 * d, BF16, rope_tables=qk_tables, seq_len=dec_seq)
            vs = _project(hs, attn_in, j, 2 * d, d, BF16)
            zs = _project(hs, attn_in, j, 3 * d, d, F32)
            ys = _attn_latent(qks.reshape(dec_batch, dec_seq, 2 * d), vs.reshape(dec_batch, dec_seq, d),
                              ck, cv, j, zs.reshape(dec_batch, dec_seq, d), attn_lambda, subln, lam_init)
            outs = _attn_finish(ys.reshape(dec_batch * dec_seq, d), xs, mod, i, lat_row, attn_out,
                                g3, b3, j, not last)
            xs, hs = outs if not last else (outs[0], None)

    return (xp.reshape(batch, seq, d), xs.reshape(dec_batch, dec_seq, d),
            k_new.reshape(batch, n_attn, seq, N_HEADS, 2, QK_DIM),
            v_new.reshape(batch, n_attn, seq, N_HEADS, HEAD_W))
```

```python
import functools
import math

import jax
import jax.numpy as jnp
from jax import lax
from jax.experimental import pallas as pl
from jax.experimental.pallas import tpu as pltpu

D_MODEL = 2048
DEPTH = 4
GRID_W = 64
N_HEADS = 16
QK_DIM = 64
HEAD_W = 2 * QK_DIM
ROPE_AXIS_DIM = QK_DIM // 2
ROPE_BASE = 10000.0
LN_EPS = 1e-5
SUBLN_EPS = 1e-5
DEEPNORM_ALPHA = (2.0 * DEPTH) ** 0.25
Q_SCALE = QK_DIM ** -0.5 * math.log2(math.e)

MOD_ROWS = 16
HALO = 16
VMEM_LIMIT_BYTES = 56 * 1024 * 1024

TM = 512
TC_CONV = 512
TM_PROJ = 1024
TN_PROJ = 1024
TN_MOD = 1024
TQ = 256
HEADS_PER_STEP = 4

F32 = jnp.float32
BF16 = jnp.bfloat16


def _cparams(sem):
    return pltpu.CompilerParams(dimension_semantics=sem, vmem_limit_bytes=VMEM_LIMIT_BYTES)


def _sigmoid(x):
    return 1.0 / (1.0 + jnp.exp(-x))


def _bdot(a, b):
    return jnp.dot(a, b, preferred_element_type=F32)


def _modulate(x, m):
    d = x.shape[-1]
    return (x * (1.0 + m[:, d:2 * d]) + m[:, :d]).astype(BF16)


def _deepnorm_ln(x, gate, o, g, b):
    r = DEEPNORM_ALPHA * x + gate * o
    mu = jnp.mean(r, axis=-1, keepdims=True)
    rc = r - mu
    var = jnp.mean(rc * rc, axis=-1, keepdims=True)
    return rc * lax.rsqrt(var + LN_EPS) * g + b


def _mod_spec(d, layer, row_of):
    return pl.BlockSpec((None, None, 1, 3 * d), lambda *idx: (layer, row_of(idx[0]), 0, 0))


def _vec_spec(d, layer):
    return pl.BlockSpec((None, 1, d), lambda *idx: (layer, 0, 0))


def _mod_kernel(c_ref, w_ref, b_ref, o_ref):
    cc = c_ref[...]
    s = (cc * _sigmoid(cc)).astype(BF16)
    o_ref[...] = _bdot(s, w_ref[...].astype(BF16)) + b_ref[...]


def _modulation(cond, ada_w, ada_b):
    d3 = ada_w.shape[-1]
    return pl.pallas_call(
        _mod_kernel,
        grid=(DEPTH, d3 // TN_MOD),
        in_specs=[
            pl.BlockSpec((MOD_ROWS, D_MODEL), lambda i, n: (0, 0)),
            pl.BlockSpec((None, D_MODEL, TN_MOD), lambda i, n: (i, 0, n)),
            pl.BlockSpec((None, 1, TN_MOD), lambda i, n: (i, 0, n)),
        ],
        out_specs=pl.BlockSpec((None, MOD_ROWS, TN_MOD), lambda i, n: (i, 0, n)),
        out_shape=jax.ShapeDtypeStruct((DEPTH, MOD_ROWS, d3), F32),
        compiler_params=_cparams(("arbitrary", "arbitrary")),
        name="adaln_modulation",
    )(cond, ada_w, ada_b.reshape(DEPTH, 1, d3))


def _conv_kernel(*refs, seq_len, n_chunks, from_x):
    if from_x:
        hp_ref, hn_ref, x_ref = refs[:3]
        h_ref = x_ref
    else:
        h_ref, hp_ref, hn_ref, x_ref = refs[:4]
    (mod_ref, modn_ref, wb_ref, wc_ref, wu_ref, wz_ref, cw_ref, wo_ref, g_ref, b_ref,
     o_ref, ho_ref, h_scr) = refs[-13:]
    t = pl.program_id(0)
    k = pl.program_id(1)
    tm, d = x_ref.shape

    @pl.when(k == 0)
    def _():
        stage = (lambda r: _modulate(r[...], mod_ref[...])) if from_x else (lambda r: r[...])
        h_scr[0:HALO, :] = stage(hp_ref)
        h_scr[HALO:HALO + tm, :] = stage(h_ref)
        h_scr[HALO + tm:, :] = stage(hn_ref)
        o_ref[...] = jnp.zeros_like(o_ref)

    h_all = h_scr[...]
    h_main = h_scr[HALO:HALO + tm, :]
    v = _bdot(h_all, wc_ref[...]) * _bdot(h_all, wu_ref[...])
    rows = v.shape[0]
    v_prev = pltpu.roll(v, 1, 0)[HALO:HALO + tm]
    v_next = pltpu.roll(v, rows - 1, 0)[HALO:HALO + tm]
    v_mid = v[HALO:HALO + tm]
    pos = (t * tm + lax.broadcasted_iota(jnp.int32, (tm, 1), 0)) % seq_len
    v_prev = jnp.where(pos == 0, 0.0, v_prev)
    v_next = jnp.where(pos == seq_len - 1, 0.0, v_next)
    cw = cw_ref[...]
    conv = v_prev * cw[0:1] + v_mid * cw[1:2] + v_next * cw[2:3]
    pz = _bdot(h_main, wz_ref[...])
    gated = conv * (pz * _sigmoid(pz))
    pb = _bdot(h_main, wb_ref[...])
    y = pb * gated
    o_ref[...] += _bdot(y.astype(BF16), wo_ref[...])

    @pl.when(k == n_chunks - 1)
    def _():
        gate = mod_ref[...][:, 2 * d:]
        xn = _deepnorm_ln(x_ref[...], gate, o_ref[...], g_ref[...], b_ref[...])
        o_ref[...] = xn
        ho_ref[...] = _modulate(xn, modn_ref[...])


def _conv_layer(h2d, x2d, mod, layer, mod_row, w_in, cw8, w_out, ln_g, ln_b, j, seq_len):
    from_x = h2d is None
    n_tok, d = x2d.shape
    n_chunks = d // TC_CONV
    hb = TM // HALO
    last_hb = n_tok // HALO - 1
    wspec = lambda g: pl.BlockSpec((None, d, TC_CONV), lambda t, k, g=g: (j, 0, g * n_chunks + k))
    tile = pl.BlockSpec((TM, d), lambda t, k: (t, 0))
    h_specs = [pl.BlockSpec((HALO, d), lambda t, k: (jnp.maximum(t * hb - 1, 0), 0)),
               pl.BlockSpec((HALO, d), lambda t, k: (jnp.minimum((t + 1) * hb, last_hb), 0))]
    h_args = [x2d, x2d] if from_x else [h2d, h2d, h2d]
    if not from_x:
        h_specs.insert(0, tile)
    return pl.pallas_call(
        functools.partial(_conv_kernel, seq_len=seq_len, n_chunks=n_chunks, from_x=from_x),
        grid=(n_tok // TM, n_chunks),
        in_specs=h_specs + [
            tile, _mod_spec(d, layer, mod_row), _mod_spec(d, layer + 1, mod_row),
            wspec(0), wspec(1), wspec(2), wspec(3),
            pl.BlockSpec((None, 8, TC_CONV), lambda t, k: (j, 0, k)),
            pl.BlockSpec((None, TC_CONV, d), lambda t, k: (j, k, 0)),
            _vec_spec(d, layer), _vec_spec(d, layer),
        ],
        out_specs=[tile, tile],
        out_shape=[jax.ShapeDtypeStruct((n_tok, d), F32), jax.ShapeDtypeStruct((n_tok, d), BF16)],
        scratch_shapes=[pltpu.VMEM((TM + 2 * HALO, d), BF16)],
        compiler_params=_cparams(("arbitrary", "arbitrary")),
        name="conv_layer",
    )(*h_args, x2d, mod, mod, w_in, w_in, w_in, w_in, cw8, w_out, ln_g, ln_b)


def _rope_head(seg, cos, sin_signed):
    lane = lax.broadcasted_iota(jnp.int32, seg.shape, 1)
    partner = jnp.where(lane % 32 < 16, pltpu.roll(seg, HEAD_W - 16, 1), pltpu.roll(seg, 16, 1))
    return seg * cos + partner * sin_signed


def _proj_kernel(*refs, rope, scale):
    h_ref, w_ref, o_ref = refs[0], refs[1], refs[-1]
    p = _bdot(h_ref[...], w_ref[...])
    if rope:
        cos, sin = refs[2][...], refs[3][...]
        for c in range(p.shape[1] // HEAD_W):
            cols = slice(c * HEAD_W, (c + 1) * HEAD_W)
            o_ref[:, cols] = _rope_head(p[:, cols], cos, sin).astype(o_ref.dtype)
    elif scale is not None:
        o_ref[...] = (p * scale).astype(o_ref.dtype)
    else:
        o_ref[...] = p.astype(o_ref.dtype).reshape(o_ref.shape)


def _project(h2d, w, j, col0, n_cols, out_dtype, *, scale=None, rope_tables=None, seq_len=None,
             cache=None):
    n_tok, d = h2d.shape
    rope = rope_tables is not None
    in_specs = [pl.BlockSpec((TM_PROJ, d), lambda t, n: (t, 0)),
                pl.BlockSpec((None, d, TN_PROJ), lambda t, n: (j, 0, col0 // TN_PROJ + n))]
    args = [h2d, w]
    aliases = {}
    if rope:
        tiles_per_seq = seq_len // TM_PROJ
        blocks_per_group = d // TN_PROJ
        tbl = pl.BlockSpec((None, TM_PROJ, HEAD_W),
                           lambda t, n: (n // blocks_per_group, t % tiles_per_seq, 0))
        in_specs += [tbl, tbl]
        args += list(rope_tables)
    if cache is None:
        out_spec = pl.BlockSpec((TM_PROJ, TN_PROJ), lambda t, n: (t, n))
        out_shape = jax.ShapeDtypeStruct((n_tok, n_cols), out_dtype)
    else:
        previous, n_layers, batch, seq = cache
        out_spec = pl.BlockSpec((TM_PROJ // seq, None, seq, TN_PROJ), lambda t, n: (t, j, 0, n))
        out_shape = jax.ShapeDtypeStruct((batch, n_layers, seq, n_cols), out_dtype)
        if previous is not None:
            in_specs.append(pl.BlockSpec(memory_space=pl.ANY))
            aliases = {len(args): 0}
            args.append(previous)
    return pl.pallas_call(
        functools.partial(_proj_kernel, rope=rope, scale=scale),
        grid=(n_tok // TM_PROJ, n_cols // TN_PROJ),
        in_specs=in_specs,
        out_specs=out_spec,
        out_shape=out_shape,
        input_output_aliases=aliases,
        compiler_params=_cparams(("arbitrary", "arbitrary")),
        name="attn_project",
    )(*args)


def _diff_lambda(lam_ref, lam_init):
    lp = lam_ref[...]
    a = jnp.sum(lp[0:1] * lp[1:2], axis=-1, keepdims=True)
    b = jnp.sum(lp[2:3] * lp[3:4], axis=-1, keepdims=True)
    return jnp.exp(a) - jnp.exp(b) + lam_init


def _scores(q, keys):
    lane = lax.broadcasted_iota(jnp.int32, q.shape, 1)
    maps = []
    for c in range(2):
        in_map = (lane < QK_DIM) if c == 0 else (lane >= QK_DIM)
        qc = jnp.where(in_map, q, jnp.zeros_like(q))
        maps.append([lax.dot_general(qc, kb, (((1,), (1,)), ((), ())), preferred_element_type=F32)
                     for kb in keys])
    return maps


def _exp_stack(maps, e_ref):
    for c in range(2):
        s = maps[c]
        tq = s[0].shape[0]
        m = functools.reduce(jnp.maximum, [jnp.max(sb, axis=-1, keepdims=True) for sb in s])
        k0 = 0
        for sb in s:
            e_ref[c * tq:(c + 1) * tq, k0:k0 + sb.shape[1]] = jnp.exp2(sb - m).astype(BF16)
            k0 += sb.shape[1]


def _combine(ov, z, lam, subln, lam_init):
    l0 = ov[0][:, HEAD_W:HEAD_W + 1]
    l1 = ov[1][:, HEAD_W:HEAD_W + 1]
    o = ov[0][:, :HEAD_W] * (1.0 / l0) - ov[1][:, :HEAD_W] * (lam / l1)
    o = o * lax.rsqrt(jnp.mean(o * o, axis=-1, keepdims=True) + SUBLN_EPS)
    o = o * subln * (1.0 - lam_init)
    return o * (z * _sigmoid(z))


def _attn_latent_kernel(q_ref, k_ref, v_ref, ck_ref, cv_ref, z_ref, lam_ref, sub_ref, y_ref,
                        e_scr, vx_scr, kc_scr, *, lam_init):
    n_heads = q_ref.shape[1] // HEAD_W
    l = k_ref.shape[0]
    cols = lambda h: slice(h * HEAD_W, (h + 1) * HEAD_W)

    @pl.when(pl.program_id(2) == 0)
    def _():
        kc_scr[...] = ck_ref[...].astype(BF16)
        for h in range(n_heads):
            vx_scr[h, 0:l, 0:HEAD_W] = v_ref[:, cols(h)]
            vx_scr[h, l:, 0:HEAD_W] = cv_ref[:, cols(h)].astype(BF16)
            vx_scr[h, :, HEAD_W:] = jnp.ones((vx_scr.shape[1], HEAD_W), BF16)

    lam = _diff_lambda(lam_ref, lam_init)
    subln = sub_ref[...]
    scores = lambda h: _scores(q_ref[:, cols(h)], [k_ref[:, cols(h)], kc_scr[:, cols(h)]])
    maps = scores(0)
    for h in range(n_heads):
        nxt = scores(h + 1) if h + 1 < n_heads else None
        e_ref = e_scr.at[h % 2]
        _exp_stack(maps, e_ref)
        tq = q_ref.shape[0]
        ov = [_bdot(e_ref[c * tq:(c + 1) * tq, :], vx_scr[h]) for c in range(2)]
        y = _combine(ov, z_ref[:, cols(h)], lam, subln, lam_init)
        y_ref[:, cols(h)] = y.astype(y_ref.dtype)
        maps = nxt


def _attn_context_kernel(q_ref, k_ref, v_ref, z_ref, lam_ref, sub_ref, y_ref, e_scr, vx_scr,
                         *, lam_init):
    lam = _diff_lambda(lam_ref, lam_init)
    subln = sub_ref[...]
    heads = range(q_ref.shape[1] // HEAD_W)
    cols = lambda h: slice(h * HEAD_W, (h + 1) * HEAD_W)
    for h in heads:
        vx_scr[h, :, 0:HEAD_W] = v_ref[:, cols(h)].astype(BF16)
        vx_scr[h, :, HEAD_W:] = jnp.ones((vx_scr.shape[1], HEAD_W), BF16)
    maps = [_scores(q_ref[:, cols(h)], [k_ref[:, cols(h)].astype(BF16)]) for h in heads]
    for h in heads:
        _exp_stack(maps[h], e_scr.at[h])
    for h in heads:
        tq = q_ref.shape[0]
        ov = _bdot(e_scr[h], vx_scr[h])
        y = _combine([ov[:tq], ov[tq:]], z_ref[:, cols(h)], lam, subln, lam_init)
        y_ref[:, cols(h)] = y.astype(y_ref.dtype)


def _attn_latent(qk, v, cache_k, cache_v, j, z, lam_p, subln, lam_init):
    b, l, hw = v.shape
    p = cache_k.shape[2]
    hb = HEADS_PER_STEP * HEAD_W
    k_off = hw // hb
    qspec = pl.BlockSpec((None, TQ, hb), lambda bi, h, qi: (bi, qi, h))
    kspec = pl.BlockSpec((None, l, hb), lambda bi, h, qi: (bi, 0, k_off + h))
    vspec = pl.BlockSpec((None, l, hb), lambda bi, h, qi: (bi, 0, h))
    cspec = pl.BlockSpec((None, None, p, hb), lambda bi, h, qi: (bi, j, 0, h))
    return pl.pallas_call(
        functools.partial(_attn_latent_kernel, lam_init=lam_init),
        grid=(b, hw // hb, l // TQ),
        in_specs=[qspec, kspec, vspec, cspec, cspec, qspec,
                  pl.BlockSpec((None, 4, QK_DIM), lambda bi, h, qi: (j, 0, 0)),
                  pl.BlockSpec((None, 1, HEAD_W), lambda bi, h, qi: (j, 0, 0))],
        out_specs=qspec,
        out_shape=jax.ShapeDtypeStruct((b, l, hw), BF16),
        scratch_shapes=[pltpu.VMEM((2, 2 * TQ, l + p), BF16),
                        pltpu.VMEM((HEADS_PER_STEP, l + p, 2 * HEAD_W), BF16),
                        pltpu.VMEM((p, hb), BF16)],
        compiler_params=_cparams(("arbitrary", "arbitrary", "arbitrary")),
        name="diff_attention_latent",
    )(qk, qk, v, cache_k, cache_v, z, lam_p, subln)


def _attn_context(q, k_new, v_new, j, z, lam_p, subln, lam_init):
    b, l, hw = q.shape
    spec = pl.BlockSpec((None, l, hw), lambda bi: (bi, 0, 0))
    kvspec = pl.BlockSpec((None, None, l, hw), lambda bi: (bi, j, 0, 0))
    n_heads = hw // HEAD_W
    return pl.pallas_call(
        functools.partial(_attn_context_kernel, lam_init=lam_init),
        grid=(b,),
        in_specs=[spec, kvspec, kvspec, spec,
                  pl.BlockSpec((None, 4, QK_DIM), lambda bi: (j, 0, 0)),
                  pl.BlockSpec((None, 1, HEAD_W), lambda bi: (j, 0, 0))],
        out_specs=spec,
        out_shape=jax.ShapeDtypeStruct((b, l, hw), BF16),
        scratch_shapes=[pltpu.VMEM((n_heads, 2 * l, l), BF16),
                        pltpu.VMEM((n_heads, l, 2 * HEAD_W), BF16)],
        compiler_params=_cparams(("arbitrary",)),
        name="diff_attention_context",
    )(q, k_new, v_new, z, lam_p, subln)


def _finish_kernel(y_ref, x_ref, mod_ref, modn_ref, w_ref, g_ref, b_ref, o_ref, *ho_ref):
    d = x_ref.shape[1]
    o = _bdot(y_ref[...], w_ref[...])
    gate = mod_ref[...][:, 2 * d:]
    xn = _deepnorm_ln(x_ref[...], gate, o, g_ref[...], b_ref[...])
    o_ref[...] = xn
    if ho_ref:
        ho_ref[0][...] = _modulate(xn, modn_ref[...])


def _attn_finish(y2d, x2d, mod, layer, mod_row, w_out, ln_g, ln_b, j, emit_h):
    n_tok, d = x2d.shape
    tile = pl.BlockSpec((TM, d), lambda t: (t, 0))
    out_specs = [tile, tile] if emit_h else [tile]
    out_shape = [jax.ShapeDtypeStruct((n_tok, d), F32), jax.ShapeDtypeStruct((n_tok, d), BF16)]
    return pl.pallas_call(
        _finish_kernel,
        grid=(n_tok // TM,),
        in_specs=[tile, tile, _mod_spec(d, layer, mod_row),
                  _mod_spec(d, min(layer + 1, DEPTH - 1), mod_row),
                  pl.BlockSpec((None, d, d), lambda t: (j, 0, 0)),
                  _vec_spec(d, layer), _vec_spec(d, layer)],
        out_specs=out_specs,
        out_shape=out_shape[:len(out_specs)],
        compiler_params=_cparams(("arbitrary",)),
        name="attn_finish",
    )(y2d, x2d, mod, mod, w_out, ln_g, ln_b)


def _rope_tables(n_tokens):
    half = ROPE_AXIS_DIM // 2
    tok = jnp.arange(n_tokens)
    row = (tok // GRID_W).astype(F32)
    col = (tok % GRID_W).astype(F32)
    inv = ROPE_BASE ** (-jnp.arange(half, dtype=F32) / half)
    lane = jnp.arange(HEAD_W)
    within = lane % ROPE_AXIS_DIM
    pos = jnp.where(((lane % QK_DIM) // ROPE_AXIS_DIM == 0)[None, :], row[:, None], col[:, None])
    ang = pos * inv[within % half][None, :]
    sign = jnp.where(within < half, -1.0, 1.0).astype(F32)
    return jnp.cos(ang), jnp.sin(ang) * sign[None, :]


def kernel(x_prompt, x_sample, cache_k, cache_v, c, c_ctx, ada_w, ada_b, ln_g, ln_b,
           conv_w_in, conv_w, conv_w_out, attn_w_in, attn_lambda, attn_subln_w, attn_w_out):
    batch, seq, d = x_prompt.shape
    dec_batch, dec_seq, _ = x_sample.shape
    n_attn, past = cache_k.shape[1], cache_k.shape[2]
    assert d == D_MODEL and dec_batch + 1 <= MOD_ROWS
    assert dec_seq % TM_PROJ == 0 and (batch * seq) % TM_PROJ == 0 and TM % seq == 0
    assert dec_seq % TQ == 0

    cond = jnp.zeros((MOD_ROWS, d), F32).at[:dec_batch].set(c).at[dec_batch].set(c_ctx)
    mod = _modulation(cond, ada_w, ada_b).reshape(DEPTH, MOD_ROWS, 1, 3 * d)

    ctx_row = lambda t: dec_batch
    tiles_per_lat = dec_seq // TM
    lat_row = lambda t: t // tiles_per_lat

    xp = x_prompt.reshape(batch * seq, d)
    xs = x_sample.reshape(dec_batch * dec_seq, d)
    ck = cache_k.reshape(dec_batch, n_attn, past, d)
    cv = cache_v.reshape(dec_batch, n_attn, past, d)
    cos, sin_signed = _rope_tables(dec_seq)
    qk_tables = (jnp.stack([cos * Q_SCALE, cos]), jnp.stack([sin_signed * Q_SCALE, sin_signed]))
    cw8 = jnp.zeros((conv_w.shape[0], 8, d), F32).at[:, :conv_w.shape[1]].set(conv_w)
    g3 = ln_g.reshape(DEPTH, 1, d)
    b3 = ln_b.reshape(DEPTH, 1, d)
    subln = attn_subln_w.reshape(n_attn, 1, HEAD_W)
    conv_in, conv_out = conv_w_in.astype(BF16), conv_w_out.astype(BF16)
    attn_in, attn_out = attn_w_in.astype(BF16), attn_w_out.astype(BF16)

    hp = hs = None
    k_new = v_new = None
    for i in range(DEPTH):
        j = i // 2
        last = i == DEPTH - 1
        if i % 2 == 0:
            xp, hp = _conv_layer(hp, xp, mod, i, ctx_row, conv_in, cw8, conv_out, g3, b3, j, seq)
            xs, hs = _conv_layer(hs, xs, mod, i, lat_row, conv_in, cw8, conv_out, g3, b3, j, dec_seq)
        else:
            lam_init = 0.8 - 0.6 * math.exp(-0.3 * i)

            qp = _project(hp, attn_in, j, 0, d, BF16, scale=Q_SCALE)
            k_new = _project(hp, attn_in, j, d, d, F32, cache=(k_new, n_attn, batch, seq))
            v_new = _project(hp, attn_in, j, 2 * d, d, F32, cache=(v_new, n_attn, batch, seq))
            zp = _project(hp, attn_in, j, 3 * d, d, F32)
            yp = _attn_context(qp.reshape(batch, seq, d), k_new, v_new, j, zp.reshape(batch, seq, d),
                               attn_lambda, subln, lam_init)
            outs = _attn_finish(yp.reshape(batch * seq, d), xp, mod, i, ctx_row, attn_out, g3, b3, j,
                                not last)
            xp, hp = outs if not last else (outs[0], None)

            qks = _project(hs, attn_in, j, 0, 2 * d, BF16, rope_tables=qk_tables, seq_len=dec_seq)
            vs = _project(hs, attn_in, j, 2 * d, d, BF16)
            zs = _project(hs, attn_in, j, 3 * d, d, F32)
            ys = _attn_latent(qks.reshape(dec_batch, dec_seq, 2 * d), vs.reshape(dec_batch, dec_seq, d),
                              ck, cv, j, zs.reshape(dec_batch, dec_seq, d), attn_lambda, subln, lam_init)
            outs = _attn_finish(ys.reshape(dec_batch * dec_seq, d), xs, mod, i, lat_row, attn_out,
                                g3, b3, j, not last)
            xs, hs = outs if not last else (outs[0], None)

    return (xp.reshape(batch, seq, d), xs.reshape(dec_batch, dec_seq, d),
            k_new.reshape(batch, n_attn, seq, N_HEADS, 2, QK_DIM),
            v_new.reshape(batch, n_attn, seq, N_HEADS, HEAD_W))
```

```python
import functools
import math

import jax
import jax.numpy as jnp
from jax import lax
from jax.experimental import pallas as pl
from jax.experimental.pallas import tpu as pltpu

D_MODEL = 2048
DEPTH = 4
GRID_W = 64
N_HEADS = 16
QK_DIM = 64
HEAD_W = 2 * QK_DIM
ROPE_AXIS_DIM = QK_DIM // 2
ROPE_BASE = 10000.0
LN_EPS = 1e-5
SUBLN_EPS = 1e-5
DEEPNORM_ALPHA = (2.0 * DEPTH) ** 0.25
Q_SCALE = QK_DIM ** -0.5 * math.log2(math.e)

MOD_ROWS = 16
HALO = 16
VMEM_LIMIT_BYTES = 56 * 1024 * 1024

TM = 512
TC_CONV = 512
TM_PROJ = 1024
TN_PROJ = 2048
TN_KT = 1024
TN_MOD = 1024
TQ = 256
HEADS_PER_STEP = 4

F32 = jnp.float32
BF16 = jnp.bfloat16


def _cparams(sem):
    return pltpu.CompilerParams(dimension_semantics=sem, vmem_limit_bytes=VMEM_LIMIT_BYTES)


def _sigmoid(x):
    return 1.0 / (1.0 + jnp.exp(-x))


def _bdot(a, b):
    return jnp.dot(a, b, preferred_element_type=F32)


def _modulate(x, m):
    d = x.shape[-1]
    return (x * (1.0 + m[:, d:2 * d]) + m[:, :d]).astype(BF16)


def _deepnorm_ln(x, gate, o, g, b):
    r = DEEPNORM_ALPHA * x + gate * o
    mu = jnp.mean(r, axis=-1, keepdims=True)
    rc = r - mu
    var = jnp.mean(rc * rc, axis=-1, keepdims=True)
    return rc * lax.rsqrt(var + LN_EPS) * g + b


def _mod_spec(d, layer, row_of):
    return pl.BlockSpec((None, None, 1, 3 * d), lambda *idx: (layer, row_of(idx[0]), 0, 0))


def _vec_spec(d, layer):
    return pl.BlockSpec((None, 1, d), lambda *idx: (layer, 0, 0))


def _mod_kernel(c_ref, w_ref, b_ref, o_ref):
    cc = c_ref[...]
    s = (cc * _sigmoid(cc)).astype(BF16)
    o_ref[...] = _bdot(s, w_ref[...].astype(BF16)) + b_ref[...]


def _modulation(cond, ada_w, ada_b):
    d3 = ada_w.shape[-1]
    return pl.pallas_call(
        _mod_kernel,
        grid=(DEPTH, d3 // TN_MOD),
        in_specs=[
            pl.BlockSpec((MOD_ROWS, D_MODEL), lambda i, n: (0, 0)),
            pl.BlockSpec((None, D_MODEL, TN_MOD), lambda i, n: (i, 0, n)),
            pl.BlockSpec((None, 1, TN_MOD), lambda i, n: (i, 0, n)),
        ],
        out_specs=pl.BlockSpec((None, MOD_ROWS, TN_MOD), lambda i, n: (i, 0, n)),
        out_shape=jax.ShapeDtypeStruct((DEPTH, MOD_ROWS, d3), F32),
        compiler_params=_cparams(("arbitrary", "arbitrary")),
        name="adaln_modulation",
    )(cond, ada_w, ada_b.reshape(DEPTH, 1, d3))


def _conv_kernel(*refs, seq_len, n_chunks, from_x):
    if from_x:
        hp_ref, hn_ref, x_ref = refs[:3]
        h_ref = x_ref
    else:
        h_ref, hp_ref, hn_ref, x_ref = refs[:4]
    (mod_ref, modn_ref, wb_ref, wc_ref, wu_ref, wz_ref, cw_ref, wo_ref, g_ref, b_ref,
     o_ref, ho_ref, h_scr) = refs[-13:]
    t = pl.program_id(0)
    k = pl.program_id(1)
    tm, d = x_ref.shape

    @pl.when(k == 0)
    def _():
        stage = (lambda r: _modulate(r[...], mod_ref[...])) if from_x else (lambda r: r[...])
        h_scr[0:HALO, :] = stage(hp_ref)
        h_scr[HALO:HALO + tm, :] = stage(h_ref)
        h_scr[HALO + tm:, :] = stage(hn_ref)
        o_ref[...] = jnp.zeros_like(o_ref)

    h_all = h_scr[...]
    h_main = h_scr[HALO:HALO + tm, :]
    v = _bdot(h_all, wc_ref[...]) * _bdot(h_all, wu_ref[...])
    rows = v.shape[0]
    v_prev = pltpu.roll(v, 1, 0)[HALO:HALO + tm]
    v_next = pltpu.roll(v, rows - 1, 0)[HALO:HALO + tm]
    v_mid = v[HALO:HALO + tm]
    pos = (t * tm + lax.broadcasted_iota(jnp.int32, (tm, 1), 0)) % seq_len
    v_prev = jnp.where(pos == 0, 0.0, v_prev)
    v_next = jnp.where(pos == seq_len - 1, 0.0, v_next)
    cw = cw_ref[...]
    conv = v_prev * cw[0:1] + v_mid * cw[1:2] + v_next * cw[2:3]
    pz = _bdot(h_main, wz_ref[...])
    gated = conv * (pz * _sigmoid(pz))
    pb = _bdot(h_main, wb_ref[...])
    y = pb * gated
    o_ref[...] += _bdot(y.astype(BF16), wo_ref[...])

    @pl.when(k == n_chunks - 1)
    def _():
        gate = mod_ref[...][:, 2 * d:]
        xn = _deepnorm_ln(x_ref[...], gate, o_ref[...], g_ref[...], b_ref[...])
        o_ref[...] = xn
        ho_ref[...] = _modulate(xn, modn_ref[...])


def _conv_layer(h2d, x2d, mod, layer, mod_row, w_in, cw8, w_out, ln_g, ln_b, j, seq_len):
    from_x = h2d is None
    n_tok, d = x2d.shape
    n_chunks = d // TC_CONV
    hb = TM // HALO
    last_hb = n_tok // HALO - 1
    wspec = lambda g: pl.BlockSpec((None, d, TC_CONV), lambda t, k, g=g: (j, 0, g * n_chunks + k))
    tile = pl.BlockSpec((TM, d), lambda t, k: (t, 0))
    h_specs = [pl.BlockSpec((HALO, d), lambda t, k: (jnp.maximum(t * hb - 1, 0), 0)),
               pl.BlockSpec((HALO, d), lambda t, k: (jnp.minimum((t + 1) * hb, last_hb), 0))]
    h_args = [x2d, x2d] if from_x else [h2d, h2d, h2d]
    if not from_x:
        h_specs.insert(0, tile)
    return pl.pallas_call(
        functools.partial(_conv_kernel, seq_len=seq_len, n_chunks=n_chunks, from_x=from_x),
        grid=(n_tok // TM, n_chunks),
        in_specs=h_specs + [
            tile, _mod_spec(d, layer, mod_row), _mod_spec(d, layer + 1, mod_row),
            wspec(0), wspec(1), wspec(2), wspec(3),
            pl.BlockSpec((None, 8, TC_CONV), lambda t, k: (j, 0, k)),
            pl.BlockSpec((None, TC_CONV, d), lambda t, k: (j, k, 0)),
            _vec_spec(d, layer), _vec_spec(d, layer),
        ],
        out_specs=[tile, tile],
        out_shape=[jax.ShapeDtypeStruct((n_tok, d), F32), jax.ShapeDtypeStruct((n_tok, d), BF16)],
        scratch_shapes=[pltpu.VMEM((TM + 2 * HALO, d), BF16)],
        compiler_params=_cparams(("arbitrary", "arbitrary")),
        name="conv_layer",
    )(*h_args, x2d, mod, mod, w_in, w_in, w_in, w_in, cw8, w_out, ln_g, ln_b)


def _rope_head(seg, cos, sin_signed):
    lane = lax.broadcasted_iota(jnp.int32, seg.shape, 1)
    partner = jnp.where(lane % 32 < 16, pltpu.roll(seg, HEAD_W - 16, 1), pltpu.roll(seg, 16, 1))
    return seg * cos + partner * sin_signed


def _proj_kernel(*refs, rope, scale):
    h_ref, w_ref, o_ref = refs[0], refs[1], refs[-1]
    p = _bdot(h_ref[...], w_ref[...])
    if rope:
        cos, sin = refs[2][...], refs[3][...]
        for c in range(p.shape[1] // HEAD_W):
            cols = slice(c * HEAD_W, (c + 1) * HEAD_W)
            o_ref[:, cols] = _rope_head(p[:, cols], cos, sin).astype(o_ref.dtype)
    elif scale is not None:
        o_ref[...] = (p * scale).astype(o_ref.dtype)
    else:
        o_ref[...] = p.astype(o_ref.dtype).reshape(o_ref.shape)


def _project(h2d, w, j, col0, n_cols, out_dtype, *, scale=None, rope_tables=None, seq_len=None,
             cache=None):
    n_tok, d = h2d.shape
    rope = rope_tables is not None
    in_specs = [pl.BlockSpec((TM_PROJ, d), lambda t, n: (t, 0)),
                pl.BlockSpec((None, d, TN_PROJ), lambda t, n: (j, 0, col0 // TN_PROJ + n))]
    args = [h2d, w]
    aliases = {}
    if rope:
        tiles_per_seq = seq_len // TM_PROJ
        blocks_per_group = d // TN_PROJ
        tbl = pl.BlockSpec((None, TM_PROJ, HEAD_W),
                           lambda t, n: (n // blocks_per_group, t % tiles_per_seq, 0))
        in_specs += [tbl, tbl]
        args += list(rope_tables)
    if cache is None:
        out_spec = pl.BlockSpec((TM_PROJ, TN_PROJ), lambda t, n: (t, n))
        out_shape = jax.ShapeDtypeStruct((n_tok, n_cols), out_dtype)
    else:
        previous, n_layers, batch, seq = cache
        out_spec = pl.BlockSpec((TM_PROJ // seq, None, seq, TN_PROJ), lambda t, n: (t, j, 0, n))
        out_shape = jax.ShapeDtypeStruct((batch, n_layers, seq, n_cols), out_dtype)
        if previous is not None:
            in_specs.append(pl.BlockSpec(memory_space=pl.ANY))
            aliases = {len(args): 0}
            args.append(previous)
    return pl.pallas_call(
        functools.partial(_proj_kernel, rope=rope, scale=scale),
        grid=(n_tok // TM_PROJ, n_cols // TN_PROJ),
        in_specs=in_specs,
        out_specs=out_spec,
        out_shape=out_shape,
        input_output_aliases=aliases,
        compiler_params=_cparams(("arbitrary", "arbitrary")),
        name="attn_project",
    )(*args)


def _proj_t_kernel(*refs):
    h_ref, wt_ref, o_ref = refs[0], refs[1], refs[-1]
    n_seq, heads, _, seq = o_ref.shape
    pt = lax.dot_general(wt_ref[...], h_ref[...], (((1,), (1,)), ((), ())), preferred_element_type=F32)
    for s in range(n_seq):
        o_ref[s] = pt[:, s * seq:(s + 1) * seq].reshape(heads, HEAD_W, seq)


def _project_keys_t(h2d, w_t, j, previous, n_layers, batch, seq):
    n_tok, d = h2d.shape
    heads_blk = TN_KT // HEAD_W
    in_specs = [pl.BlockSpec((TM_PROJ, d), lambda t, n: (t, 0)),
                pl.BlockSpec((None, TN_KT, d), lambda t, n: (j, n, 0))]
    args = [h2d, w_t]
    aliases = {}
    if previous is not None:
        in_specs.append(pl.BlockSpec(memory_space=pl.ANY))
        aliases = {len(args): 0}
        args.append(previous)
    return pl.pallas_call(
        _proj_t_kernel,
        grid=(n_tok // TM_PROJ, w_t.shape[1] // TN_KT),
        in_specs=in_specs,
        out_specs=pl.BlockSpec((TM_PROJ // seq, None, heads_blk, HEAD_W, seq),
                               lambda t, n: (t, j, n, 0, 0)),
        out_shape=jax.ShapeDtypeStruct((batch, n_layers, w_t.shape[1] // HEAD_W, HEAD_W, seq), F32),
        input_output_aliases=aliases,
        compiler_params=_cparams(("arbitrary", "arbitrary")),
        name="attn_project_keys_t",
    )(*args)


def _diff_lambda(lam_ref, lam_init):
    lp = lam_ref[...]
    a = jnp.sum(lp[0:1] * lp[1:2], axis=-1, keepdims=True)
    b = jnp.sum(lp[2:3] * lp[3:4], axis=-1, keepdims=True)
    return jnp.exp(a) - jnp.exp(b) + lam_init


def _scores(q, keys, keys_t=()):
    lane = lax.broadcasted_iota(jnp.int32, q.shape, 1)
    maps = []
    for c in range(2):
        in_map = (lane < QK_DIM) if c == 0 else (lane >= QK_DIM)
        qc = jnp.where(in_map, q, jnp.zeros_like(q))
        maps.append([lax.dot_general(qc, kb, (((1,), (1,)), ((), ())), preferred_element_type=F32)
                     for kb in keys] + [_bdot(qc, kt) for kt in keys_t])
    return maps


def _exp_stack(maps, e_ref):
    for c in range(2):
        s = maps[c]
        tq = s[0].shape[0]
        m = functools.reduce(jnp.maximum, [jnp.max(sb, axis=-1, keepdims=True) for sb in s])
        k0 = 0
        for sb in s:
            e_ref[c * tq:(c + 1) * tq, k0:k0 + sb.shape[1]] = jnp.exp2(sb - m).astype(BF16)
            k0 += sb.shape[1]


def _combine(ov, z, lam, subln, lam_init):
    l0 = ov[0][:, HEAD_W:HEAD_W + 1]
    l1 = ov[1][:, HEAD_W:HEAD_W + 1]
    o = ov[0][:, :HEAD_W] * (1.0 / l0) - ov[1][:, :HEAD_W] * (lam / l1)
    o = o * lax.rsqrt(jnp.mean(o * o, axis=-1, keepdims=True) + SUBLN_EPS)
    o = o * subln * (1.0 - lam_init)
    return o * (z * _sigmoid(z))


def _attn_latent_kernel(q_ref, k_ref, v_ref, ck_ref, cv_ref, z_ref, lam_ref, sub_ref, y_ref,
                        e_scr, vx_scr, kc_scr, *, lam_init):
    n_heads = q_ref.shape[1] // HEAD_W
    l = k_ref.shape[0]
    cols = lambda h: slice(h * HEAD_W, (h + 1) * HEAD_W)

    @pl.when(pl.program_id(2) == 0)
    def _():
        kc_scr[...] = ck_ref[...].astype(BF16)
        for h in range(n_heads):
            vx_scr[h, 0:l, 0:HEAD_W] = v_ref[:, cols(h)]
            vx_scr[h, l:, 0:HEAD_W] = cv_ref[:, cols(h)].astype(BF16)
            vx_scr[h, :, HEAD_W:] = jnp.ones((vx_scr.shape[1], HEAD_W), BF16)

    lam = _diff_lambda(lam_ref, lam_init)
    subln = sub_ref[...]
    scores = lambda h: _scores(q_ref[:, cols(h)], [k_ref[:, cols(h)]], [kc_scr[h]])
    maps = scores(0)
    for h in range(n_heads):
        nxt = scores(h + 1) if h + 1 < n_heads else None
        e_ref = e_scr.at[h % 2]
        _exp_stack(maps, e_ref)
        tq = q_ref.shape[0]
        ov = [_bdot(e_ref[c * tq:(c + 1) * tq, :], vx_scr[h]) for c in range(2)]
        y = _combine(ov, z_ref[:, cols(h)], lam, subln, lam_init)
        y_ref[:, cols(h)] = y.astype(y_ref.dtype)
        maps = nxt


def _attn_context_kernel(q_ref, kt_ref, v_ref, z_ref, lam_ref, sub_ref, y_ref, e_scr, vx_scr,
                         *, lam_init):
    lam = _diff_lambda(lam_ref, lam_init)
    subln = sub_ref[...]
    heads = range(q_ref.shape[1] // HEAD_W)
    cols = lambda h: slice(h * HEAD_W, (h + 1) * HEAD_W)
    for h in heads:
        vx_scr[h, :, 0:HEAD_W] = v_ref[:, cols(h)].astype(BF16)
        vx_scr[h, :, HEAD_W:] = jnp.ones((vx_scr.shape[1], HEAD_W), BF16)
    maps = [_scores(q_ref[:, cols(h)], [], [kt_ref[h].astype(BF16)]) for h in heads]
    for h in heads:
        _exp_stack(maps[h], e_scr.at[h])
    for h in heads:
        tq = q_ref.shape[0]
        ov = _bdot(e_scr[h], vx_scr[h])
        y = _combine([ov[:tq], ov[tq:]], z_ref[:, cols(h)], lam, subln, lam_init)
        y_ref[:, cols(h)] = y.astype(y_ref.dtype)


def _attn_latent(qk, v, cache_kt, cache_v, j, z, lam_p, subln, lam_init):
    b, l, hw = v.shape
    p = cache_v.shape[2]
    hb = HEADS_PER_STEP * HEAD_W
    k_off = hw // hb
    qspec = pl.BlockSpec((None, TQ, hb), lambda bi, h, qi: (bi, qi, h))
    kspec = pl.BlockSpec((None, l, hb), lambda bi, h, qi: (bi, 0, k_off + h))
    vspec = pl.BlockSpec((None, l, hb), lambda bi, h, qi: (bi, 0, h))
    ckspec = pl.BlockSpec((None, None, HEADS_PER_STEP, HEAD_W, p), lambda bi, h, qi: (bi, j, h, 0, 0))
    cspec = pl.BlockSpec((None, None, p, hb), lambda bi, h, qi: (bi, j, 0, h))
    return pl.pallas_call(
        functools.partial(_attn_latent_kernel, lam_init=lam_init),
        grid=(b, hw // hb, l // TQ),
        in_specs=[qspec, kspec, vspec, ckspec, cspec, qspec,
                  pl.BlockSpec((None, 4, QK_DIM), lambda bi, h, qi: (j, 0, 0)),
                  pl.BlockSpec((None, 1, HEAD_W), lambda bi, h, qi: (j, 0, 0))],
        out_specs=qspec,
        out_shape=jax.ShapeDtypeStruct((b, l, hw), BF16),
        scratch_shapes=[pltpu.VMEM((2, 2 * TQ, l + p), BF16),
                        pltpu.VMEM((HEADS_PER_STEP, l + p, 2 * HEAD_W), BF16),
                        pltpu.VMEM((HEADS_PER_STEP, HEAD_W, p), BF16)],
        compiler_params=_cparams(("arbitrary", "arbitrary", "arbitrary")),
        name="diff_attention_latent",
    )(qk, qk, v, cache_kt, cache_v, z, lam_p, subln)


def _attn_context(q, kt_new, v_new, j, z, lam_p, subln, lam_init):
    b, l, hw = q.shape
    spec = pl.BlockSpec((None, l, hw), lambda bi: (bi, 0, 0))
    n_heads = hw // HEAD_W
    ktspec = pl.BlockSpec((None, None, n_heads, HEAD_W, l), lambda bi: (bi, j, 0, 0, 0))
    kvspec = pl.BlockSpec((None, None, l, hw), lambda bi: (bi, j, 0, 0))
    return pl.pallas_call(
        functools.partial(_attn_context_kernel, lam_init=lam_init),
        grid=(b,),
        in_specs=[spec, ktspec, kvspec, spec,
                  pl.BlockSpec((None, 4, QK_DIM), lambda bi: (j, 0, 0)),
                  pl.BlockSpec((None, 1, HEAD_W), lambda bi: (j, 0, 0))],
        out_specs=spec,
        out_shape=jax.ShapeDtypeStruct((b, l, hw), BF16),
        scratch_shapes=[pltpu.VMEM((n_heads, 2 * l, l), BF16),
                        pltpu.VMEM((n_heads, l, 2 * HEAD_W), BF16)],
        compiler_params=_cparams(("arbitrary",)),
        name="diff_attention_context",
    )(q, kt_new, v_new, z, lam_p, subln)


def _finish_kernel(y_ref, x_ref, mod_ref, modn_ref, w_ref, g_ref, b_ref, o_ref, *ho_ref):
    d = x_ref.shape[1]
    o = _bdot(y_ref[...], w_ref[...])
    gate = mod_ref[...][:, 2 * d:]
    xn = _deepnorm_ln(x_ref[...], gate, o, g_ref[...], b_ref[...])
    o_ref[...] = xn
    if ho_ref:
        ho_ref[0][...] = _modulate(xn, modn_ref[...])


def _attn_finish(y2d, x2d, mod, layer, mod_row, w_out, ln_g, ln_b, j, emit_h):
    n_tok, d = x2d.shape
    tile = pl.BlockSpec((TM, d), lambda t: (t, 0))
    out_specs = [tile, tile] if emit_h else [tile]
    out_shape = [jax.ShapeDtypeStruct((n_tok, d), F32), jax.ShapeDtypeStruct((n_tok, d), BF16)]
    return pl.pallas_call(
        _finish_kernel,
        grid=(n_tok // TM,),
        in_specs=[tile, tile, _mod_spec(d, layer, mod_row),
                  _mod_spec(d, min(layer + 1, DEPTH - 1), mod_row),
                  pl.BlockSpec((None, d, d), lambda t: (j, 0, 0)),
                  _vec_spec(d, layer), _vec_spec(d, layer)],
        out_specs=out_specs,
        out_shape=out_shape[:len(out_specs)],
        compiler_params=_cparams(("arbitrary",)),
        name="attn_finish",
    )(y2d, x2d, mod, mod, w_out, ln_g, ln_b)


def _rope_tables(n_tokens):
    half = ROPE_AXIS_DIM // 2
    tok = jnp.arange(n_tokens)
    row = (tok // GRID_W).astype(F32)
    col = (tok % GRID_W).astype(F32)
    inv = ROPE_BASE ** (-jnp.arange(half, dtype=F32) / half)
    lane = jnp.arange(HEAD_W)
    within = lane % ROPE_AXIS_DIM
    pos = jnp.where(((lane % QK_DIM) // ROPE_AXIS_DIM == 0)[None, :], row[:, None], col[:, None])
    ang = pos * inv[within % half][None, :]
    sign = jnp.where(within < half, -1.0, 1.0).astype(F32)
    return jnp.cos(ang), jnp.sin(ang) * sign[None, :]


def kernel(x_prompt, x_sample, cache_k, cache_v, c, c_ctx, ada_w, ada_b, ln_g, ln_b,
           conv_w_in, conv_w, conv_w_out, attn_w_in, attn_lambda, attn_subln_w, attn_w_out):
    batch, seq, d = x_prompt.shape
    dec_batch, dec_seq, _ = x_sample.shape
    n_attn, past = cache_k.shape[1], cache_k.shape[2]
    assert d == D_MODEL and dec_batch + 1 <= MOD_ROWS
    assert dec_seq % TM_PROJ == 0 and (batch * seq) % TM_PROJ == 0 and TM % seq == 0
    assert dec_seq % TQ == 0

    cond = jnp.zeros((MOD_ROWS, d), F32).at[:dec_batch].set(c).at[dec_batch].set(c_ctx)
    mod = _modulation(cond, ada_w, ada_b).reshape(DEPTH, MOD_ROWS, 1, 3 * d)

    ctx_row = lambda t: dec_batch
    tiles_per_lat = dec_seq // TM
    lat_row = lambda t: t // tiles_per_lat

    xp = x_prompt.reshape(batch * seq, d)
    xs = x_sample.reshape(dec_batch * dec_seq, d)
    ck = cache_k.transpose(0, 1, 3, 4, 5, 2).reshape(dec_batch, n_attn, N_HEADS, HEAD_W, past)
    cv = cache_v.reshape(dec_batch, n_attn, past, d)
    cos, sin_signed = _rope_tables(dec_seq)
    qk_tables = (jnp.stack([cos * Q_SCALE, cos]), jnp.stack([sin_signed * Q_SCALE, sin_signed]))
    cw8 = jnp.zeros((conv_w.shape[0], 8, d), F32).at[:, :conv_w.shape[1]].set(conv_w)
    g3 = ln_g.reshape(DEPTH, 1, d)
    b3 = ln_b.reshape(DEPTH, 1, d)
    subln = attn_subln_w.reshape(n_attn, 1, HEAD_W)
    conv_in, conv_out = conv_w_in.astype(BF16), conv_w_out.astype(BF16)
    attn_in, attn_out = attn_w_in.astype(BF16), attn_w_out.astype(BF16)

    attn_k_t = jnp.swapaxes(attn_in[:, :, d:2 * d], 1, 2)

    hp = hs = None
    kt_new = v_new = None
    for i in range(DEPTH):
        j = i // 2
        last = i == DEPTH - 1
        if i % 2 == 0:
            xp, hp = _conv_layer(hp, xp, mod, i, ctx_row, conv_in, cw8, conv_out, g3, b3, j, seq)
            xs, hs = _conv_layer(hs, xs, mod, i, lat_row, conv_in, cw8, conv_out, g3, b3, j, dec_seq)
        else:
            lam_init = 0.8 - 0.6 * math.exp(-0.3 * i)

            qp = _project(hp, attn_in, j, 0, d, BF16, scale=Q_SCALE)
            kt_new = _project_keys_t(hp, attn_k_t, j, kt_new, n_attn, batch, seq)
            v_new = _project(hp, attn_in, j, 2 * d, d, F32, cache=(v_new, n_attn, batch, seq))
            zp = _project(hp, attn_in, j, 3 * d, d, F32)
            yp = _attn_context(qp.reshape(batch, seq, d), kt_new, v_new, j, zp.reshape(batch, seq, d),
                               attn_lambda, subln, lam_init)
            outs = _attn_finish(yp.reshape(batch * seq, d), xp, mod, i, ctx_row, attn_out, g3, b3, j,
                                not last)
            xp, hp = outs if not last else (outs[0], None)

            qks = _project(hs, attn_in, j, 0, 2 * d, BF16, rope_tables=qk_tables, seq_len=dec_seq)
            vs = _project(hs, attn_in, j, 2 * d, d, BF16)
            zs = _project(hs, attn_in, j, 3 * d, d, F32)
            ys = _attn_latent(qks.reshape(dec_batch, dec_seq, 2 * d), vs.reshape(dec_batch, dec_seq, d),
                              ck, cv, j, zs.reshape(dec_batch, dec_seq, d), attn_lambda, subln, lam_init)
            outs = _attn_finish(ys.reshape(dec_batch * dec_seq, d), xs, mod, i, lat_row, attn_out,
                                g3, b3, j, not last)
            xs, hs = outs if not last else (outs[0], None)

    return (xp.reshape(batch, seq, d), xs.reshape(dec_batch, dec_seq, d),
            kt_new.reshape(batch, n_attn, N_HEADS, 2, QK_DIM, seq).transpose(0, 1, 5, 2, 3, 4),
            v_new.reshape(batch, n_attn, seq, N_HEADS, HEAD_W))
```

```python
import functools
import math

import jax
import jax.numpy as jnp
from jax import lax
from jax.experimental import pallas as pl
from jax.experimental.pallas import tpu as pltpu

D_MODEL = 2048
DEPTH = 4
GRID_W = 64
N_HEADS = 16
QK_DIM = 64
HEAD_W = 2 * QK_DIM
ROPE_AXIS_DIM = QK_DIM // 2
ROPE_BASE = 10000.0
LN_EPS = 1e-5
SUBLN_EPS = 1e-5
DEEPNORM_ALPHA = (2.0 * DEPTH) ** 0.25
Q_SCALE = QK_DIM ** -0.5 * math.log2(math.e)

MOD_ROWS = 16
HALO = 16
VMEM_LIMIT_BYTES = 56 * 1024 * 1024

TM = 512
TC_CONV = 512
TM_PROJ = 1024
TN_PROJ = 2048
TN_CACHE = 1024
TN_KT = 1024
TN_MOD = 1024
TQ = 256
HEADS_PER_STEP = 4

F32 = jnp.float32
BF16 = jnp.bfloat16


def _cparams(sem):
    return pltpu.CompilerParams(dimension_semantics=sem, vmem_limit_bytes=VMEM_LIMIT_BYTES)


def _sigmoid(x):
    return 1.0 / (1.0 + jnp.exp(-x))


def _bdot(a, b):
    return jnp.dot(a, b, preferred_element_type=F32)


def _modulate(x, m):
    d = x.shape[-1]
    return (x * (1.0 + m[:, d:2 * d]) + m[:, :d]).astype(BF16)


def _deepnorm_ln(x, gate, o, g, b):
    r = DEEPNORM_ALPHA * x + gate * o
    mu = jnp.mean(r, axis=-1, keepdims=True)
    rc = r - mu
    var = jnp.mean(rc * rc, axis=-1, keepdims=True)
    return rc * lax.rsqrt(var + LN_EPS) * g + b


def _mod_spec(d, layer, row_of):
    return pl.BlockSpec((None, None, 1, 3 * d), lambda *idx: (layer, row_of(idx[0]), 0, 0))


def _vec_spec(d, layer):
    return pl.BlockSpec((None, 1, d), lambda *idx: (layer, 0, 0))


def _mod_kernel(c_ref, w_ref, b_ref, o_ref):
    cc = c_ref[...]
    s = (cc * _sigmoid(cc)).astype(BF16)
    o_ref[...] = _bdot(s, w_ref[...].astype(BF16)) + b_ref[...]


def _modulation(cond, ada_w, ada_b):
    d3 = ada_w.shape[-1]
    return pl.pallas_call(
        _mod_kernel,
        grid=(DEPTH, d3 // TN_MOD),
        in_specs=[
            pl.BlockSpec((MOD_ROWS, D_MODEL), lambda i, n: (0, 0)),
            pl.BlockSpec((None, D_MODEL, TN_MOD), lambda i, n: (i, 0, n)),
            pl.BlockSpec((None, 1, TN_MOD), lambda i, n: (i, 0, n)),
        ],
        out_specs=pl.BlockSpec((None, MOD_ROWS, TN_MOD), lambda i, n: (i, 0, n)),
        out_shape=jax.ShapeDtypeStruct((DEPTH, MOD_ROWS, d3), F32),
        compiler_params=_cparams(("arbitrary", "arbitrary")),
        name="adaln_modulation",
    )(cond, ada_w, ada_b.reshape(DEPTH, 1, d3))


def _conv_kernel(*refs, seq_len, n_chunks, from_x):
    if from_x:
        hp_ref, hn_ref, x_ref = refs[:3]
        h_ref = x_ref
    else:
        h_ref, hp_ref, hn_ref, x_ref = refs[:4]
    (mod_ref, modn_ref, wb_ref, wc_ref, wu_ref, wz_ref, cw_ref, wo_ref, g_ref, b_ref,
     o_ref, ho_ref, h_scr) = refs[-13:]
    t = pl.program_id(0)
    k = pl.program_id(1)
    tm, d = x_ref.shape

    @pl.when(k == 0)
    def _():
        stage = (lambda r: _modulate(r[...], mod_ref[...])) if from_x else (lambda r: r[...])
        h_scr[0:HALO, :] = stage(hp_ref)
        h_scr[HALO:HALO + tm, :] = stage(h_ref)
        h_scr[HALO + tm:, :] = stage(hn_ref)
        o_ref[...] = jnp.zeros_like(o_ref)

    h_all = h_scr[...]
    h_main = h_scr[HALO:HALO + tm, :]
    v = _bdot(h_all, wc_ref[...]) * _bdot(h_all, wu_ref[...])
    rows = v.shape[0]
    v_prev = pltpu.roll(v, 1, 0)[HALO:HALO + tm]
    v_next = pltpu.roll(v, rows - 1, 0)[HALO:HALO + tm]
    v_mid = v[HALO:HALO + tm]
    pos = (t * tm + lax.broadcasted_iota(jnp.int32, (tm, 1), 0)) % seq_len
    v_prev = jnp.where(pos == 0, 0.0, v_prev)
    v_next = jnp.where(pos == seq_len - 1, 0.0, v_next)
    cw = cw_ref[...]
    conv = v_prev * cw[0:1] + v_mid * cw[1:2] + v_next * cw[2:3]
    pz = _bdot(h_main, wz_ref[...])
    gated = conv * (pz * _sigmoid(pz))
    pb = _bdot(h_main, wb_ref[...])
    y = pb * gated
    o_ref[...] += _bdot(y.astype(BF16), wo_ref[...])

    @pl.when(k == n_chunks - 1)
    def _():
        gate = mod_ref[...][:, 2 * d:]
        xn = _deepnorm_ln(x_ref[...], gate, o_ref[...], g_ref[...], b_ref[...])
        o_ref[...] = xn
        ho_ref[...] = _modulate(xn, modn_ref[...])


def _conv_layer(h2d, x2d, mod, layer, mod_row, w_in, cw8, w_out, ln_g, ln_b, j, seq_len):
    from_x = h2d is None
    n_tok, d = x2d.shape
    n_chunks = d // TC_CONV
    hb = TM // HALO
    last_hb = n_tok // HALO - 1
    wspec = lambda g: pl.BlockSpec((None, d, TC_CONV), lambda t, k, g=g: (j, 0, g * n_chunks + k))
    tile = pl.BlockSpec((TM, d), lambda t, k: (t, 0))
    h_specs = [pl.BlockSpec((HALO, d), lambda t, k: (jnp.maximum(t * hb - 1, 0), 0)),
               pl.BlockSpec((HALO, d), lambda t, k: (jnp.minimum((t + 1) * hb, last_hb), 0))]
    h_args = [x2d, x2d] if from_x else [h2d, h2d, h2d]
    if not from_x:
        h_specs.insert(0, tile)
    return pl.pallas_call(
        functools.partial(_conv_kernel, seq_len=seq_len, n_chunks=n_chunks, from_x=from_x),
        grid=(n_tok // TM, n_chunks),
        in_specs=h_specs + [
            tile, _mod_spec(d, layer, mod_row), _mod_spec(d, layer + 1, mod_row),
            wspec(0), wspec(1), wspec(2), wspec(3),
            pl.BlockSpec((None, 8, TC_CONV), lambda t, k: (j, 0, k)),
            pl.BlockSpec((None, TC_CONV, d), lambda t, k: (j, k, 0)),
            _vec_spec(d, layer), _vec_spec(d, layer),
        ],
        out_specs=[tile, tile],
        out_shape=[jax.ShapeDtypeStruct((n_tok, d), F32), jax.ShapeDtypeStruct((n_tok, d), BF16)],
        scratch_shapes=[pltpu.VMEM((TM + 2 * HALO, d), BF16)],
        compiler_params=_cparams(("arbitrary", "arbitrary")),
        name="conv_layer",
    )(*h_args, x2d, mod, mod, w_in, w_in, w_in, w_in, cw8, w_out, ln_g, ln_b)


def _rope_head(seg, cos, sin_signed):
    lane = lax.broadcasted_iota(jnp.int32, seg.shape, 1)
    partner = jnp.where(lane % 32 < 16, pltpu.roll(seg, HEAD_W - 16, 1), pltpu.roll(seg, 16, 1))
    return seg * cos + partner * sin_signed


def _store_layer(o_ref, value, layer):
    if layer is None:
        o_ref[...] = value
    else:
        for li in range(o_ref.shape[1]):
            o_ref[:, li] = value if li == layer else jnp.zeros_like(value)


def _proj_kernel(*refs, rope, scale, fill_layer):
    h_ref, w_ref, o_ref = refs[0], refs[1], refs[-1]
    p = _bdot(h_ref[...], w_ref[...])
    if rope:
        cos, sin = refs[2][...], refs[3][...]
        for c in range(p.shape[1] // HEAD_W):
            cols = slice(c * HEAD_W, (c + 1) * HEAD_W)
            o_ref[:, cols] = _rope_head(p[:, cols], cos, sin).astype(o_ref.dtype)
    elif scale is not None:
        o_ref[...] = (p * scale).astype(o_ref.dtype)
    elif len(o_ref.shape) == 2:
        o_ref[...] = p.astype(o_ref.dtype)
    else:
        n_seq, seq = o_ref.shape[0], o_ref.shape[-2]
        _store_layer(o_ref, p.astype(o_ref.dtype).reshape(n_seq, seq, p.shape[1]), fill_layer)


def _project(h2d, w, j, col0, n_cols, out_dtype, *, scale=None, rope_tables=None, seq_len=None,
             cache=None):
    n_tok, d = h2d.shape
    rope = rope_tables is not None
    tn = TN_PROJ if cache is None else TN_CACHE
    in_specs = [pl.BlockSpec((TM_PROJ, d), lambda t, n: (t, 0)),
                pl.BlockSpec((None, d, tn), lambda t, n: (j, 0, col0 // tn + n))]
    args = [h2d, w]
    aliases = {}
    fill_layer = None
    if rope:
        tiles_per_seq = seq_len // TM_PROJ
        blocks_per_group = d // tn
        tbl = pl.BlockSpec((None, TM_PROJ, HEAD_W),
                           lambda t, n: (n // blocks_per_group, t % tiles_per_seq, 0))
        in_specs += [tbl, tbl]
        args += list(rope_tables)
    if cache is None:
        out_spec = pl.BlockSpec((TM_PROJ, tn), lambda t, n: (t, n))
        out_shape = jax.ShapeDtypeStruct((n_tok, n_cols), out_dtype)
    else:
        previous, n_layers, batch, seq = cache
        out_shape = jax.ShapeDtypeStruct((batch, n_layers, seq, n_cols), out_dtype)
        if previous is None:
            fill_layer = j
            out_spec = pl.BlockSpec((TM_PROJ // seq, n_layers, seq, tn), lambda t, n: (t, 0, 0, n))
        else:
            out_spec = pl.BlockSpec((TM_PROJ // seq, None, seq, tn), lambda t, n: (t, j, 0, n))
            in_specs.append(pl.BlockSpec(memory_space=pl.ANY))
            aliases = {len(args): 0}
            args.append(previous)
    return pl.pallas_call(
        functools.partial(_proj_kernel, rope=rope, scale=scale, fill_layer=fill_layer),
        grid=(n_tok // TM_PROJ, n_cols // tn),
        in_specs=in_specs,
        out_specs=out_spec,
        out_shape=out_shape,
        input_output_aliases=aliases,
        compiler_params=_cparams(("arbitrary", "arbitrary")),
        name="attn_project",
    )(*args)


def _proj_t_kernel(*refs, fill_layer):
    h_ref, wt_ref, o_ref = refs[0], refs[1], refs[-1]
    n_seq, heads, seq = o_ref.shape[0], o_ref.shape[-3], o_ref.shape[-1]
    pt = lax.dot_general(wt_ref[...], h_ref[...], (((1,), (1,)), ((), ())), preferred_element_type=F32)
    for s in range(n_seq):
        _store_layer(o_ref.at[pl.ds(s, 1)],
                     pt[:, s * seq:(s + 1) * seq].reshape(1, heads, HEAD_W, seq), fill_layer)


def _project_keys_t(h2d, w_t, j, previous, n_layers, batch, seq):
    n_tok, d = h2d.shape
    heads_blk = TN_KT // HEAD_W
    in_specs = [pl.BlockSpec((TM_PROJ, d), lambda t, n: (t, 0)),
                pl.BlockSpec((None, TN_KT, d), lambda t, n: (j, n, 0))]
    args = [h2d, w_t]
    aliases = {}
    if previous is None:
        fill_layer = j
        out_spec = pl.BlockSpec((TM_PROJ // seq, n_layers, heads_blk, HEAD_W, seq),
                                lambda t, n: (t, 0, n, 0, 0))
    else:
        fill_layer = None
        out_spec = pl.BlockSpec((TM_PROJ // seq, None, heads_blk, HEAD_W, seq),
                                lambda t, n: (t, j, n, 0, 0))
        in_specs.append(pl.BlockSpec(memory_space=pl.ANY))
        aliases = {len(args): 0}
        args.append(previous)
    return pl.pallas_call(
        functools.partial(_proj_t_kernel, fill_layer=fill_layer),
        grid=(n_tok // TM_PROJ, w_t.shape[1] // TN_KT),
        in_specs=in_specs,
        out_specs=out_spec,
        out_shape=jax.ShapeDtypeStruct((batch, n_layers, w_t.shape[1] // HEAD_W, HEAD_W, seq), F32),
        input_output_aliases=aliases,
        compiler_params=_cparams(("arbitrary", "arbitrary")),
        name="attn_project_keys_t",
    )(*args)


def _diff_lambda(lam_ref, lam_init):
    lp = lam_ref[...]
    a = jnp.sum(lp[0:1] * lp[1:2], axis=-1, keepdims=True)
    b = jnp.sum(lp[2:3] * lp[3:4], axis=-1, keepdims=True)
    return jnp.exp(a) - jnp.exp(b) + lam_init


def _scores(q, keys, keys_t=()):
    lane = lax.broadcasted_iota(jnp.int32, q.shape, 1)
    maps = []
    for c in range(2):
        in_map = (lane < QK_DIM) if c == 0 else (lane >= QK_DIM)
        qc = jnp.where(in_map, q, jnp.zeros_like(q))
        maps.append([lax.dot_general(qc, kb, (((1,), (1,)), ((), ())), preferred_element_type=F32)
                     for kb in keys] + [_bdot(qc, kt) for kt in keys_t])
    return maps


def _exp_stack(maps, e_ref):
    for c in range(2):
        s = maps[c]
        tq = s[0].shape[0]
        m = functools.reduce(jnp.maximum, [jnp.max(sb, axis=-1, keepdims=True) for sb in s])
        k0 = 0
        for sb in s:
            e_ref[c * tq:(c + 1) * tq, k0:k0 + sb.shape[1]] = jnp.exp2(sb - m).astype(BF16)
            k0 += sb.shape[1]


def _combine(ov, z, lam, subln, lam_init):
    l0 = ov[0][:, HEAD_W:HEAD_W + 1]
    l1 = ov[1][:, HEAD_W:HEAD_W + 1]
    o = ov[0][:, :HEAD_W] * (1.0 / l0) - ov[1][:, :HEAD_W] * (lam / l1)
    o = o * lax.rsqrt(jnp.mean(o * o, axis=-1, keepdims=True) + SUBLN_EPS)
    o = o * subln * (1.0 - lam_init)
    return o * (z * _sigmoid(z))


def _attn_latent_kernel(q_ref, k_ref, v_ref, ck_ref, cv_ref, z_ref, lam_ref, sub_ref, y_ref,
                        e_scr, vx_scr, kc_scr, *, lam_init):
    n_heads = q_ref.shape[1] // HEAD_W
    l = k_ref.shape[0]
    cols = lambda h: slice(h * HEAD_W, (h + 1) * HEAD_W)

    @pl.when(pl.program_id(2) == 0)
    def _():
        kc_scr[...] = ck_ref[...].astype(BF16)
        for h in range(n_heads):
            vx_scr[h, 0:l, 0:HEAD_W] = v_ref[:, cols(h)]
            vx_scr[h, l:, 0:HEAD_W] = cv_ref[:, cols(h)].astype(BF16)
            vx_scr[h, :, HEAD_W:] = jnp.ones((vx_scr.shape[1], HEAD_W), BF16)

    lam = _diff_lambda(lam_ref, lam_init)
    subln = sub_ref[...]
    scores = lambda h: _scores(q_ref[:, cols(h)], [k_ref[:, cols(h)]], [kc_scr[h]])
    maps = scores(0)
    for h in range(n_heads):
        nxt = scores(h + 1) if h + 1 < n_heads else None
        e_ref = e_scr.at[h % 2]
        _exp_stack(maps, e_ref)
        tq = q_ref.shape[0]
        ov = [_bdot(e_ref[c * tq:(c + 1) * tq, :], vx_scr[h]) for c in range(2)]
        y = _combine(ov, z_ref[:, cols(h)], lam, subln, lam_init)
        y_ref[:, cols(h)] = y.astype(y_ref.dtype)
        maps = nxt


def _attn_context_kernel(q_ref, kt_ref, v_ref, z_ref, lam_ref, sub_ref, y_ref, e_scr, vx_scr,
                         ov_scr, *, lam_init):
    lam = _diff_lambda(lam_ref, lam_init)
    subln = sub_ref[...]
    heads = range(q_ref.shape[1] // HEAD_W)
    cols = lambda h: slice(h * HEAD_W, (h + 1) * HEAD_W)
    for h in heads:
        vx_scr[h, :, 0:HEAD_W] = v_ref[:, cols(h)].astype(BF16)
        vx_scr[h, :, HEAD_W:] = jnp.ones((vx_scr.shape[1], HEAD_W), BF16)
    maps = [_scores(q_ref[:, cols(h)], [], [kt_ref[h].astype(BF16)]) for h in heads]
    for h in heads:
        _exp_stack(maps[h], e_scr.at[h])
    for h in heads:
        ov_scr[h] = _bdot(e_scr[h], vx_scr[h])
    tq = q_ref.shape[0]
    l0 = ov_scr[:, :tq, HEAD_W:HEAD_W + 1]
    l1 = ov_scr[:, tq:, HEAD_W:HEAD_W + 1]
    o = ov_scr[:, :tq, :HEAD_W] * (1.0 / l0) - ov_scr[:, tq:, :HEAD_W] * (lam / l1)
    o = o * lax.rsqrt(jnp.mean(o * o, axis=-1, keepdims=True) + SUBLN_EPS)
    o = o * subln * (1.0 - lam_init)
    for h in heads:
        z = z_ref[:, cols(h)]
        y_ref[:, cols(h)] = (o[h] * (z * _sigmoid(z))).astype(y_ref.dtype)


def _attn_latent(qk, v, cache_kt, cache_v, j, z, lam_p, subln, lam_init):
    b, l, hw = v.shape
    p = cache_v.shape[2]
    hb = HEADS_PER_STEP * HEAD_W
    k_off = hw // hb
    qspec = pl.BlockSpec((None, TQ, hb), lambda bi, h, qi: (bi, qi, h))
    kspec = pl.BlockSpec((None, l, hb), lambda bi, h, qi: (bi, 0, k_off + h))
    vspec = pl.BlockSpec((None, l, hb), lambda bi, h, qi: (bi, 0, h))
    ckspec = pl.BlockSpec((None, None, HEADS_PER_STEP, HEAD_W, p), lambda bi, h, qi: (bi, j, h, 0, 0))
    cspec = pl.BlockSpec((None, None, p, hb), lambda bi, h, qi: (bi, j, 0, h))
    return pl.pallas_call(
        functools.partial(_attn_latent_kernel, lam_init=lam_init),
        grid=(b, hw // hb, l // TQ),
        in_specs=[qspec, kspec, vspec, ckspec, cspec, qspec,
                  pl.BlockSpec((None, 4, QK_DIM), lambda bi, h, qi: (j, 0, 0)),
                  pl.BlockSpec((None, 1, HEAD_W), lambda bi, h, qi: (j, 0, 0))],
        out_specs=qspec,
        out_shape=jax.ShapeDtypeStruct((b, l, hw), BF16),
        scratch_shapes=[pltpu.VMEM((2, 2 * TQ, l + p), BF16),
                        pltpu.VMEM((HEADS_PER_STEP, l + p, 2 * HEAD_W), BF16),
                        pltpu.VMEM((HEADS_PER_STEP, HEAD_W, p), BF16)],
        compiler_params=_cparams(("arbitrary", "arbitrary", "arbitrary")),
        name="diff_attention_latent",
    )(qk, qk, v, cache_kt, cache_v, z, lam_p, subln)


def _attn_context(q, kt_new, v_new, j, z, lam_p, subln, lam_init):
    b, l, hw = q.shape
    spec = pl.BlockSpec((None, l, hw), lambda bi: (bi, 0, 0))
    n_heads = hw // HEAD_W
    ktspec = pl.BlockSpec((None, None, n_heads, HEAD_W, l), lambda bi: (bi, j, 0, 0, 0))
    kvspec = pl.BlockSpec((None, None, l, hw), lambda bi: (bi, j, 0, 0))
    return pl.pallas_call(
        functools.partial(_attn_context_kernel, lam_init=lam_init),
        grid=(b,),
        in_specs=[spec, ktspec, kvspec, spec,
                  pl.BlockSpec((None, 4, QK_DIM), lambda bi: (j, 0, 0)),
                  pl.BlockSpec((None, 1, HEAD_W), lambda bi: (j, 0, 0))],
        out_specs=spec,
        out_shape=jax.ShapeDtypeStruct((b, l, hw), BF16),
        scratch_shapes=[pltpu.VMEM((n_heads, 2 * l, l), BF16),
                        pltpu.VMEM((n_heads, l, 2 * HEAD_W), BF16),
                        pltpu.VMEM((n_heads, 2 * l, 2 * HEAD_W), F32)],
        compiler_params=_cparams(("arbitrary",)),
        name="diff_attention_context",
    )(q, kt_new, v_new, z, lam_p, subln)


def _finish_kernel(y_ref, x_ref, mod_ref, modn_ref, w_ref, g_ref, b_ref, o_ref, *ho_ref):
    tm, d = x_ref.shape
    gate = mod_ref[...][:, 2 * d:]
    for rows in (slice(0, tm // 2), slice(tm // 2, tm)):
        o = _bdot(y_ref[rows, :], w_ref[...])
        xn = _deepnorm_ln(x_ref[rows, :], gate, o, g_ref[...], b_ref[...])
        o_ref[rows, :] = xn
        if ho_ref:
            ho_ref[0][rows, :] = _modulate(xn, modn_ref[...])


def _attn_finish(y2d, x2d, mod, layer, mod_row, w_out, ln_g, ln_b, j, emit_h):
    n_tok, d = x2d.shape
    tile = pl.BlockSpec((TM, d), lambda t: (t, 0))
    out_specs = [tile, tile] if emit_h else [tile]
    out_shape = [jax.ShapeDtypeStruct((n_tok, d), F32), jax.ShapeDtypeStruct((n_tok, d), BF16)]
    return pl.pallas_call(
        _finish_kernel,
        grid=(n_tok // TM,),
        in_specs=[tile, tile, _mod_spec(d, layer, mod_row),
                  _mod_spec(d, min(layer + 1, DEPTH - 1), mod_row),
                  pl.BlockSpec((None, d, d), lambda t: (j, 0, 0)),
                  _vec_spec(d, layer), _vec_spec(d, layer)],
        out_specs=out_specs,
        out_shape=out_shape[:len(out_specs)],
        compiler_params=_cparams(("arbitrary",)),
        name="attn_finish",
    )(y2d, x2d, mod, mod, w_out, ln_g, ln_b)


def _rope_tables(n_tokens):
    half = ROPE_AXIS_DIM // 2
    tok = jnp.arange(n_tokens)
    row = (tok // GRID_W).astype(F32)
    col = (tok % GRID_W).astype(F32)
    inv = ROPE_BASE ** (-jnp.arange(half, dtype=F32) / half)
    lane = jnp.arange(HEAD_W)
    within = lane % ROPE_AXIS_DIM
    pos = jnp.where(((lane % QK_DIM) // ROPE_AXIS_DIM == 0)[None, :], row[:, None], col[:, None])
    ang = pos * inv[within % half][None, :]
    sign = jnp.where(within < half, -1.0, 1.0).astype(F32)
    return jnp.cos(ang), jnp.sin(ang) * sign[None, :]


def kernel(x_prompt, x_sample, cache_k, cache_v, c, c_ctx, ada_w, ada_b, ln_g, ln_b,
           conv_w_in, conv_w, conv_w_out, attn_w_in, attn_lambda, attn_subln_w, attn_w_out):
    batch, seq, d = x_prompt.shape
    dec_batch, dec_seq, _ = x_sample.shape
    n_attn, past = cache_k.shape[1], cache_k.shape[2]
    assert d == D_MODEL and dec_batch + 1 <= MOD_ROWS
    assert dec_seq % TM_PROJ == 0 and (batch * seq) % TM_PROJ == 0 and TM % seq == 0
    assert dec_seq % TQ == 0

    cond = jnp.zeros((MOD_ROWS, d), F32).at[:dec_batch].set(c).at[dec_batch].set(c_ctx)
    mod = _modulation(cond, ada_w, ada_b).reshape(DEPTH, MOD_ROWS, 1, 3 * d)

    ctx_row = lambda t: dec_batch
    tiles_per_lat = dec_seq // TM
    lat_row = lambda t: t // tiles_per_lat

    xp = x_prompt.reshape(batch * seq, d)
    xs = x_sample.reshape(dec_batch * dec_seq, d)
    ck = cache_k.transpose(0, 1, 3, 4, 5, 2).reshape(dec_batch, n_attn, N_HEADS, HEAD_W, past)
    cv = cache_v.reshape(dec_batch, n_attn, past, d)
    cos, sin_signed = _rope_tables(dec_seq)
    qk_tables = (jnp.stack([cos * Q_SCALE, cos]), jnp.stack([sin_signed * Q_SCALE, sin_signed]))
    cw8 = jnp.zeros((conv_w.shape[0], 8, d), F32).at[:, :conv_w.shape[1]].set(conv_w)
    g3 = ln_g.reshape(DEPTH, 1, d)
    b3 = ln_b.reshape(DEPTH, 1, d)
    subln = attn_subln_w.reshape(n_attn, 1, HEAD_W)
    conv_in, conv_out = conv_w_in.astype(BF16), conv_w_out.astype(BF16)
    attn_in, attn_out = attn_w_in.astype(BF16), attn_w_out.astype(BF16)

    attn_k_t = jnp.swapaxes(attn_in[:, :, d:2 * d], 1, 2)

    hp = hs = None
    kt_new = v_new = None
    for i in range(DEPTH):
        j = i // 2
        last = i == DEPTH - 1
        if i % 2 == 0:
            xp, hp = _conv_layer(hp, xp, mod, i, ctx_row, conv_in, cw8, conv_out, g3, b3, j, seq)
            xs, hs = _conv_layer(hs, xs, mod, i, lat_row, conv_in, cw8, conv_out, g3, b3, j, dec_seq)
        else:
            lam_init = 0.8 - 0.6 * math.exp(-0.3 * i)

            qp = _project(hp, attn_in, j, 0, d, BF16, scale=Q_SCALE)
            kt_new = _project_keys_t(hp, attn_k_t, j, kt_new, n_attn, batch, seq)
            v_new = _project(hp, attn_in, j, 2 * d, d, F32, cache=(v_new, n_attn, batch, seq))
            zp = _project(hp, attn_in, j, 3 * d, d, F32)
            yp = _attn_context(qp.reshape(batch, seq, d), kt_new, v_new, j, zp.reshape(batch, seq, d),
                               attn_lambda, subln, lam_init)
            outs = _attn_finish(yp.reshape(batch * seq, d), xp, mod, i, ctx_row, attn_out, g3, b3, j,
                                not last)
            xp, hp = outs if not last else (outs[0], None)

            qks = _project(hs, attn_in, j, 0, 2 * d, BF16, rope_tables=qk_tables, seq_len=dec_seq)
            vs = _project(hs, attn_in, j, 2 * d, d, BF16)
            zs = _project(hs, attn_in, j, 3 * d, d, F32)
            ys = _attn_latent(qks.reshape(dec_batch, dec_seq, 2 * d), vs.reshape(dec_batch, dec_seq, d),
                              ck, cv, j, zs.reshape(dec_batch, dec_seq, d), attn_lambda, subln, lam_init)
            outs = _attn_finish(ys.reshape(dec_batch * dec_seq, d), xs, mod, i, lat_row, attn_out,
                                g3, b3, j, not last)
            xs, hs = outs if not last else (outs[0], None)

    return (xp.reshape(batch, seq, d), xs.reshape(dec_batch, dec_seq, d),
            kt_new.reshape(batch, n_attn, N_HEADS, 2, QK_DIM, seq).transpose(0, 1, 5, 2, 3, 4),
            v_new.reshape(batch, n_attn, seq, N_HEADS, HEAD_W))
```

```python
import functools
import math

import jax
import jax.numpy as jnp
from jax import lax
from jax.experimental import pallas as pl
from jax.experimental.pallas import tpu as pltpu

D_MODEL = 2048
DEPTH = 4
GRID_W = 64
N_HEADS = 16
QK_DIM = 64
HEAD_W = 2 * QK_DIM
ROPE_AXIS_DIM = QK_DIM // 2
ROPE_BASE = 10000.0
LN_EPS = 1e-5
SUBLN_EPS = 1e-5
DEEPNORM_ALPHA = (2.0 * DEPTH) ** 0.25
Q_SCALE = QK_DIM ** -0.5 * math.log2(math.e)

MOD_ROWS = 16
HALO = 16
VMEM_LIMIT_BYTES = 56 * 1024 * 1024

TM = 512
TC_CONV = 512
TM_PROJ = 1024
TN_PROJ = 2048
TN_CACHE = 1024
TN_KT = 1024
TN_MOD = 1024
TQ = 256
HEADS_PER_STEP = 8

F32 = jnp.float32
BF16 = jnp.bfloat16


def _cparams(sem):
    return pltpu.CompilerParams(dimension_semantics=sem, vmem_limit_bytes=VMEM_LIMIT_BYTES)


def _sigmoid(x):
    return 1.0 / (1.0 + jnp.exp(-x))


def _bdot(a, b):
    return jnp.dot(a, b, preferred_element_type=F32)


def _modulate(x, m):
    d = x.shape[-1]
    return (x * (1.0 + m[:, d:2 * d]) + m[:, :d]).astype(BF16)


def _deepnorm_ln(x, gate, o, g, b):
    r = DEEPNORM_ALPHA * x + gate * o
    mu = jnp.mean(r, axis=-1, keepdims=True)
    rc = r - mu
    var = jnp.mean(rc * rc, axis=-1, keepdims=True)
    return rc * lax.rsqrt(var + LN_EPS) * g + b


def _mod_spec(d, layer, row_of):
    return pl.BlockSpec((None, None, 1, 3 * d), lambda *idx: (layer, row_of(idx[0]), 0, 0))


def _vec_spec(d, layer):
    return pl.BlockSpec((None, 1, d), lambda *idx: (layer, 0, 0))


def _mod_kernel(c_ref, w_ref, b_ref, o_ref):
    cc = c_ref[...]
    s = (cc * _sigmoid(cc)).astype(BF16)
    o_ref[...] = _bdot(s, w_ref[...].astype(BF16)) + b_ref[...]


def _modulation(cond, ada_w, ada_b):
    d3 = ada_w.shape[-1]
    return pl.pallas_call(
        _mod_kernel,
        grid=(DEPTH, d3 // TN_MOD),
        in_specs=[
            pl.BlockSpec((MOD_ROWS, D_MODEL), lambda i, n: (0, 0)),
            pl.BlockSpec((None, D_MODEL, TN_MOD), lambda i, n: (i, 0, n)),
            pl.BlockSpec((None, 1, TN_MOD), lambda i, n: (i, 0, n)),
        ],
        out_specs=pl.BlockSpec((None, MOD_ROWS, TN_MOD), lambda i, n: (i, 0, n)),
        out_shape=jax.ShapeDtypeStruct((DEPTH, MOD_ROWS, d3), F32),
        compiler_params=_cparams(("arbitrary", "arbitrary")),
        name="adaln_modulation",
    )(cond, ada_w, ada_b.reshape(DEPTH, 1, d3))


def _conv_kernel(*refs, seq_len, n_chunks, from_x):
    if from_x:
        hp_ref, hn_ref, x_ref = refs[:3]
        h_ref = x_ref
    else:
        h_ref, hp_ref, hn_ref, x_ref = refs[:4]
    (mod_ref, modn_ref, wb_ref, wc_ref, wu_ref, wz_ref, cw_ref, wo_ref, g_ref, b_ref,
     o_ref, ho_ref, h_scr) = refs[-13:]
    t = pl.program_id(0)
    k = pl.program_id(1)
    tm, d = x_ref.shape

    @pl.when(k == 0)
    def _():
        stage = (lambda r: _modulate(r[...], mod_ref[...])) if from_x else (lambda r: r[...])
        h_scr[0:HALO, :] = stage(hp_ref)
        h_scr[HALO:HALO + tm, :] = stage(h_ref)
        h_scr[HALO + tm:, :] = stage(hn_ref)
        o_ref[...] = jnp.zeros_like(o_ref)

    h_all = h_scr[...]
    h_main = h_scr[HALO:HALO + tm, :]
    v = _bdot(h_all, wc_ref[...]) * _bdot(h_all, wu_ref[...])
    rows = v.shape[0]
    v_prev = pltpu.roll(v, 1, 0)[HALO:HALO + tm]
    v_next = pltpu.roll(v, rows - 1, 0)[HALO:HALO + tm]
    v_mid = v[HALO:HALO + tm]
    pos = (t * tm + lax.broadcasted_iota(jnp.int32, (tm, 1), 0)) % seq_len
    v_prev = jnp.where(pos == 0, 0.0, v_prev)
    v_next = jnp.where(pos == seq_len - 1, 0.0, v_next)
    cw = cw_ref[...]
    conv = v_prev * cw[0:1] + v_mid * cw[1:2] + v_next * cw[2:3]
    pz = _bdot(h_main, wz_ref[...])
    gated = conv * (pz * _sigmoid(pz))
    pb = _bdot(h_main, wb_ref[...])
    y = pb * gated
    o_ref[...] += _bdot(y.astype(BF16), wo_ref[...])

    @pl.when(k == n_chunks - 1)
    def _():
        gate = mod_ref[...][:, 2 * d:]
        xn = _deepnorm_ln(x_ref[...], gate, o_ref[...], g_ref[...], b_ref[...])
        o_ref[...] = xn
        ho_ref[...] = _modulate(xn, modn_ref[...])


def _conv_layer(h2d, x2d, mod, layer, mod_row, w_in, cw8, w_out, ln_g, ln_b, j, seq_len):
    from_x = h2d is None
    n_tok, d = x2d.shape
    n_chunks = d // TC_CONV
    hb = TM // HALO
    last_hb = n_tok // HALO - 1
    wspec = lambda g: pl.BlockSpec((None, d, TC_CONV), lambda t, k, g=g: (j, 0, g * n_chunks + k))
    tile = pl.BlockSpec((TM, d), lambda t, k: (t, 0))
    h_specs = [pl.BlockSpec((HALO, d), lambda t, k: (jnp.maximum(t * hb - 1, 0), 0)),
               pl.BlockSpec((HALO, d), lambda t, k: (jnp.minimum((t + 1) * hb, last_hb), 0))]
    h_args = [x2d, x2d] if from_x else [h2d, h2d, h2d]
    if not from_x:
        h_specs.insert(0, tile)
    return pl.pallas_call(
        functools.partial(_conv_kernel, seq_len=seq_len, n_chunks=n_chunks, from_x=from_x),
        grid=(n_tok // TM, n_chunks),
        in_specs=h_specs + [
            tile, _mod_spec(d, layer, mod_row), _mod_spec(d, layer + 1, mod_row),
            wspec(0), wspec(1), wspec(2), wspec(3),
            pl.BlockSpec((None, 8, TC_CONV), lambda t, k: (j, 0, k)),
            pl.BlockSpec((None, TC_CONV, d), lambda t, k: (j, k, 0)),
            _vec_spec(d, layer), _vec_spec(d, layer),
        ],
        out_specs=[tile, tile],
        out_shape=[jax.ShapeDtypeStruct((n_tok, d), F32), jax.ShapeDtypeStruct((n_tok, d), BF16)],
        scratch_shapes=[pltpu.VMEM((TM + 2 * HALO, d), BF16)],
        compiler_params=_cparams(("arbitrary", "arbitrary")),
        name="conv_layer",
    )(*h_args, x2d, mod, mod, w_in, w_in, w_in, w_in, cw8, w_out, ln_g, ln_b)


def _rope_head(seg, cos, sin_signed):
    lane = lax.broadcasted_iota(jnp.int32, seg.shape, 1)
    partner = jnp.where(lane % 32 < 16, pltpu.roll(seg, HEAD_W - 16, 1), pltpu.roll(seg, 16, 1))
    return seg * cos + partner * sin_signed


def _store_layer(o_ref, value, layer):
    if layer is None:
        o_ref[...] = value
    else:
        for li in range(o_ref.shape[1]):
            o_ref[:, li] = value if li == layer else jnp.zeros_like(value)


def _proj_kernel(*refs, rope, scale, fill_layer):
    h_ref, w_ref, o_ref = refs[0], refs[1], refs[-1]
    p = _bdot(h_ref[...], w_ref[...])
    if rope:
        cos, sin = refs[2][...], refs[3][...]
        for c in range(p.shape[1] // HEAD_W):
            cols = slice(c * HEAD_W, (c + 1) * HEAD_W)
            o_ref[:, cols] = _rope_head(p[:, cols], cos, sin).astype(o_ref.dtype)
    elif scale is not None:
        o_ref[...] = (p * scale).astype(o_ref.dtype)
    elif len(o_ref.shape) == 2:
        o_ref[...] = p.astype(o_ref.dtype)
    else:
        n_seq, seq = o_ref.shape[0], o_ref.shape[-2]
        _store_layer(o_ref, p.astype(o_ref.dtype).reshape(n_seq, seq, p.shape[1]), fill_layer)


def _project(h2d, w, j, col0, n_cols, out_dtype, *, scale=None, rope_tables=None, seq_len=None,
             cache=None):
    n_tok, d = h2d.shape
    rope = rope_tables is not None
    tn = TN_PROJ if cache is None else TN_CACHE
    in_specs = [pl.BlockSpec((TM_PROJ, d), lambda t, n: (t, 0)),
                pl.BlockSpec((None, d, tn), lambda t, n: (j, 0, col0 // tn + n))]
    args = [h2d, w]
    aliases = {}
    fill_layer = None
    if rope:
        tiles_per_seq = seq_len // TM_PROJ
        blocks_per_group = d // tn
        tbl = pl.BlockSpec((None, TM_PROJ, HEAD_W),
                           lambda t, n: (n // blocks_per_group, t % tiles_per_seq, 0))
        in_specs += [tbl, tbl]
        args += list(rope_tables)
    if cache is None:
        out_spec = pl.BlockSpec((TM_PROJ, tn), lambda t, n: (t, n))
        out_shape = jax.ShapeDtypeStruct((n_tok, n_cols), out_dtype)
    else:
        previous, n_layers, batch, seq = cache
        out_shape = jax.ShapeDtypeStruct((batch, n_layers, seq, n_cols), out_dtype)
        if previous is None:
            fill_layer = j
            out_spec = pl.BlockSpec((TM_PROJ // seq, n_layers, seq, tn), lambda t, n: (t, 0, 0, n))
        else:
            out_spec = pl.BlockSpec((TM_PROJ // seq, None, seq, tn), lambda t, n: (t, j, 0, n))
            in_specs.append(pl.BlockSpec(memory_space=pl.ANY))
            aliases = {len(args): 0}
            args.append(previous)
    return pl.pallas_call(
        functools.partial(_proj_kernel, rope=rope, scale=scale, fill_layer=fill_layer),
        grid=(n_tok // TM_PROJ, n_cols // tn),
        in_specs=in_specs,
        out_specs=out_spec,
        out_shape=out_shape,
        input_output_aliases=aliases,
        compiler_params=_cparams(("arbitrary", "arbitrary")),
        name="attn_project",
    )(*args)


def _proj_t_kernel(*refs, fill_layer):
    h_ref, wt_ref, o_ref = refs[0], refs[1], refs[-1]
    n_seq, heads, seq = o_ref.shape[0], o_ref.shape[-3], o_ref.shape[-1]
    pt = lax.dot_general(wt_ref[...], h_ref[...], (((1,), (1,)), ((), ())), preferred_element_type=F32)
    for s in range(n_seq):
        _store_layer(o_ref.at[pl.ds(s, 1)],
                     pt[:, s * seq:(s + 1) * seq].reshape(1, heads, HEAD_W, seq), fill_layer)


def _project_keys_t(h2d, w_t, j, previous, n_layers, batch, seq):
    n_tok, d = h2d.shape
    heads_blk = TN_KT // HEAD_W
    in_specs = [pl.BlockSpec((TM_PROJ, d), lambda t, n: (t, 0)),
                pl.BlockSpec((None, TN_KT, d), lambda t, n: (j, n, 0))]
    args = [h2d, w_t]
    aliases = {}
    if previous is None:
        fill_layer = j
        out_spec = pl.BlockSpec((TM_PROJ // seq, n_layers, heads_blk, HEAD_W, seq),
                                lambda t, n: (t, 0, n, 0, 0))
    else:
        fill_layer = None
        out_spec = pl.BlockSpec((TM_PROJ // seq, None, heads_blk, HEAD_W, seq),
                                lambda t, n: (t, j, n, 0, 0))
        in_specs.append(pl.BlockSpec(memory_space=pl.ANY))
        aliases = {len(args): 0}
        args.append(previous)
    return pl.pallas_call(
        functools.partial(_proj_t_kernel, fill_layer=fill_layer),
        grid=(n_tok // TM_PROJ, w_t.shape[1] // TN_KT),
        in_specs=in_specs,
        out_specs=out_spec,
        out_shape=jax.ShapeDtypeStruct((batch, n_layers, w_t.shape[1] // HEAD_W, HEAD_W, seq), F32),
        input_output_aliases=aliases,
        compiler_params=_cparams(("arbitrary", "arbitrary")),
        name="attn_project_keys_t",
    )(*args)


def _diff_lambda(lam_ref, lam_init):
    lp = lam_ref[...]
    a = jnp.sum(lp[0:1] * lp[1:2], axis=-1, keepdims=True)
    b = jnp.sum(lp[2:3] * lp[3:4], axis=-1, keepdims=True)
    return jnp.exp(a) - jnp.exp(b) + lam_init


def _scores(q, keys, keys_t=()):
    lane = lax.broadcasted_iota(jnp.int32, q.shape, 1)
    maps = []
    for c in range(2):
        in_map = (lane < QK_DIM) if c == 0 else (lane >= QK_DIM)
        qc = jnp.where(in_map, q, jnp.zeros_like(q))
        maps.append([lax.dot_general(qc, kb, (((1,), (1,)), ((), ())), preferred_element_type=F32)
                     for kb in keys] + [_bdot(qc, kt) for kt in keys_t])
    return maps


def _exp_stack(maps, e_ref):
    for c in range(2):
        s = maps[c]
        tq = s[0].shape[0]
        m = functools.reduce(jnp.maximum, [jnp.max(sb, axis=-1, keepdims=True) for sb in s])
        k0 = 0
        for sb in s:
            e_ref[c * tq:(c + 1) * tq, k0:k0 + sb.shape[1]] = jnp.exp2(sb - m).astype(BF16)
            k0 += sb.shape[1]


def _combine(ov, z, lam, subln, lam_init):
    l0 = ov[0][:, HEAD_W:HEAD_W + 1]
    l1 = ov[1][:, HEAD_W:HEAD_W + 1]
    o = ov[0][:, :HEAD_W] * (1.0 / l0) - ov[1][:, :HEAD_W] * (lam / l1)
    o = o * lax.rsqrt(jnp.mean(o * o, axis=-1, keepdims=True) + SUBLN_EPS)
    o = o * subln * (1.0 - lam_init)
    return o * (z * _sigmoid(z))


def _attn_latent_kernel(q_ref, k_ref, v_ref, ck_ref, cv_ref, z_ref, lam_ref, sub_ref, y_ref,
                        e_scr, vx_scr, kc_scr, *, lam_init):
    n_heads = q_ref.shape[1] // HEAD_W
    l = k_ref.shape[0]
    cols = lambda h: slice(h * HEAD_W, (h + 1) * HEAD_W)

    @pl.when(pl.program_id(2) == 0)
    def _():
        kc_scr[...] = ck_ref[...].astype(BF16)
        for h in range(n_heads):
            vx_scr[h, 0:l, 0:HEAD_W] = v_ref[:, cols(h)]
            vx_scr[h, l:, 0:HEAD_W] = cv_ref[:, cols(h)].astype(BF16)
            vx_scr[h, :, HEAD_W:] = jnp.ones((vx_scr.shape[1], HEAD_W), BF16)

    lam = _diff_lambda(lam_ref, lam_init)
    subln = sub_ref[...]
    scores = lambda h: _scores(q_ref[:, cols(h)], [k_ref[:, cols(h)]], [kc_scr[h]])
    maps = scores(0)
    for h in range(n_heads):
        nxt = scores(h + 1) if h + 1 < n_heads else None
        e_ref = e_scr.at[h % 2]
        _exp_stack(maps, e_ref)
        tq = q_ref.shape[0]
        ov = [_bdot(e_ref[c * tq:(c + 1) * tq, :], vx_scr[h]) for c in range(2)]
        y = _combine(ov, z_ref[:, cols(h)], lam, subln, lam_init)
        y_ref[:, cols(h)] = y.astype(y_ref.dtype)
        maps = nxt


def _attn_context_kernel(q_ref, kt_ref, v_ref, z_ref, lam_ref, sub_ref, y_ref, e_scr, vx_scr,
                         ov_scr, *, lam_init):
    lam = _diff_lambda(lam_ref, lam_init)
    subln = sub_ref[...]
    heads = range(q_ref.shape[1] // HEAD_W)
    cols = lambda h: slice(h * HEAD_W, (h + 1) * HEAD_W)
    for h in heads:
        vx_scr[h, :, 0:HEAD_W] = v_ref[:, cols(h)].astype(BF16)
        vx_scr[h, :, HEAD_W:] = jnp.ones((vx_scr.shape[1], HEAD_W), BF16)
    maps = [_scores(q_ref[:, cols(h)], [], [kt_ref[h].astype(BF16)]) for h in heads]
    for h in heads:
        _exp_stack(maps[h], e_scr.at[h])
    for h in heads:
        ov_scr[h] = _bdot(e_scr[h], vx_scr[h])
    tq = q_ref.shape[0]
    l0 = ov_scr[:, :tq, HEAD_W:HEAD_W + 1]
    l1 = ov_scr[:, tq:, HEAD_W:HEAD_W + 1]
    o = ov_scr[:, :tq, :HEAD_W] * (1.0 / l0) - ov_scr[:, tq:, :HEAD_W] * (lam / l1)
    o = o * lax.rsqrt(jnp.mean(o * o, axis=-1, keepdims=True) + SUBLN_EPS)
    o = o * subln * (1.0 - lam_init)
    for h in heads:
        z = z_ref[:, cols(h)]
        y_ref[:, cols(h)] = (o[h] * (z * _sigmoid(z))).astype(y_ref.dtype)


def _attn_latent(qk, v, cache_kt, cache_v, j, z, lam_p, subln, lam_init):
    b, l, hw = v.shape
    p = cache_v.shape[2]
    hb = HEADS_PER_STEP * HEAD_W
    k_off = hw // hb
    qspec = pl.BlockSpec((None, TQ, hb), lambda bi, h, qi: (bi, qi, h))
    kspec = pl.BlockSpec((None, l, hb), lambda bi, h, qi: (bi, 0, k_off + h))
    vspec = pl.BlockSpec((None, l, hb), lambda bi, h, qi: (bi, 0, h))
    ckspec = pl.BlockSpec((None, None, HEADS_PER_STEP, HEAD_W, p), lambda bi, h, qi: (bi, j, h, 0, 0))
    cspec = pl.BlockSpec((None, None, p, hb), lambda bi, h, qi: (bi, j, 0, h))
    return pl.pallas_call(
        functools.partial(_attn_latent_kernel, lam_init=lam_init),
        grid=(b, hw // hb, l // TQ),
        in_specs=[qspec, kspec, vspec, ckspec, cspec, qspec,
                  pl.BlockSpec((None, 4, QK_DIM), lambda bi, h, qi: (j, 0, 0)),
                  pl.BlockSpec((None, 1, HEAD_W), lambda bi, h, qi: (j, 0, 0))],
        out_specs=qspec,
        out_shape=jax.ShapeDtypeStruct((b, l, hw), BF16),
        scratch_shapes=[pltpu.VMEM((2, 2 * TQ, l + p), BF16),
                        pltpu.VMEM((HEADS_PER_STEP, l + p, 2 * HEAD_W), BF16),
                        pltpu.VMEM((HEADS_PER_STEP, HEAD_W, p), BF16)],
        compiler_params=_cparams(("arbitrary", "arbitrary", "arbitrary")),
        name="diff_attention_latent",
    )(qk, qk, v, cache_kt, cache_v, z, lam_p, subln)


def _attn_context(q, kt_new, v_new, j, z, lam_p, subln, lam_init):
    b, l, hw = q.shape
    spec = pl.BlockSpec((None, l, hw), lambda bi: (bi, 0, 0))
    n_heads = hw // HEAD_W
    ktspec = pl.BlockSpec((None, None, n_heads, HEAD_W, l), lambda bi: (bi, j, 0, 0, 0))
    kvspec = pl.BlockSpec((None, None, l, hw), lambda bi: (bi, j, 0, 0))
    return pl.pallas_call(
        functools.partial(_attn_context_kernel, lam_init=lam_init),
        grid=(b,),
        in_specs=[spec, ktspec, kvspec, spec,
                  pl.BlockSpec((None, 4, QK_DIM), lambda bi: (j, 0, 0)),
                  pl.BlockSpec((None, 1, HEAD_W), lambda bi: (j, 0, 0))],
        out_specs=spec,
        out_shape=jax.ShapeDtypeStruct((b, l, hw), BF16),
        scratch_shapes=[pltpu.VMEM((n_heads, 2 * l, l), BF16),
                        pltpu.VMEM((n_heads, l, 2 * HEAD_W), BF16),
                        pltpu.VMEM((n_heads, 2 * l, 2 * HEAD_W), F32)],
        compiler_params=_cparams(("arbitrary",)),
        name="diff_attention_context",
    )(q, kt_new, v_new, z, lam_p, subln)


def _finish_kernel(y_ref, x_ref, mod_ref, modn_ref, w_ref, g_ref, b_ref, o_ref, *ho_ref):
    d = x_ref.shape[1]
    o = _bdot(y_ref[...], w_ref[...])
    gate = mod_ref[...][:, 2 * d:]
    xn = _deepnorm_ln(x_ref[...], gate, o, g_ref[...], b_ref[...])
    o_ref[...] = xn
    if ho_ref:
        ho_ref[0][...] = _modulate(xn, modn_ref[...])


def _attn_finish(y2d, x2d, mod, layer, mod_row, w_out, ln_g, ln_b, j, emit_h):
    n_tok, d = x2d.shape
    tile = pl.BlockSpec((TM, d), lambda t: (t, 0))
    out_specs = [tile, tile] if emit_h else [tile]
    out_shape = [jax.ShapeDtypeStruct((n_tok, d), F32), jax.ShapeDtypeStruct((n_tok, d), BF16)]
    return pl.pallas_call(
        _finish_kernel,
        grid=(n_tok // TM,),
        in_specs=[tile, tile, _mod_spec(d, layer, mod_row),
                  _mod_spec(d, min(layer + 1, DEPTH - 1), mod_row),
                  pl.BlockSpec((None, d, d), lambda t: (j, 0, 0)),
                  _vec_spec(d, layer), _vec_spec(d, layer)],
        out_specs=out_specs,
        out_shape=out_shape[:len(out_specs)],
        compiler_params=_cparams(("arbitrary",)),
        name="attn_finish",
    )(y2d, x2d, mod, mod, w_out, ln_g, ln_b)


def _rope_tables(n_tokens):
    half = ROPE_AXIS_DIM // 2
    tok = jnp.arange(n_tokens)
    row = (tok // GRID_W).astype(F32)
    col = (tok % GRID_W).astype(F32)
    inv = ROPE_BASE ** (-jnp.arange(half, dtype=F32) / half)
    lane = jnp.arange(HEAD_W)
    within = lane % ROPE_AXIS_DIM
    pos = jnp.where(((lane % QK_DIM) // ROPE_AXIS_DIM == 0)[None, :], row[:, None], col[:, None])
    ang = pos * inv[within % half][None, :]
    sign = jnp.where(within < half, -1.0, 1.0).astype(F32)
    return jnp.cos(ang), jnp.sin(ang) * sign[None, :]


def kernel(x_prompt, x_sample, cache_k, cache_v, c, c_ctx, ada_w, ada_b, ln_g, ln_b,
           conv_w_in, conv_w, conv_w_out, attn_w_in, attn_lambda, attn_subln_w, attn_w_out):
    batch, seq, d = x_prompt.shape
    dec_batch, dec_seq, _ = x_sample.shape
    n_attn, past = cache_k.shape[1], cache_k.shape[2]
    assert d == D_MODEL and dec_batch + 1 <= MOD_ROWS
    assert dec_seq % TM_PROJ == 0 and (batch * seq) % TM_PROJ == 0 and TM % seq == 0
    assert dec_seq % TQ == 0

    cond = jnp.zeros((MOD_ROWS, d), F32).at[:dec_batch].set(c).at[dec_batch].set(c_ctx)
    mod = _modulation(cond, ada_w, ada_b).reshape(DEPTH, MOD_ROWS, 1, 3 * d)

    ctx_row = lambda t: dec_batch
    tiles_per_lat = dec_seq // TM
    lat_row = lambda t: t // tiles_per_lat

    xp = x_prompt.reshape(batch * seq, d)
    xs = x_sample.reshape(dec_batch * dec_seq, d)
    ck = cache_k.transpose(0, 1, 3, 4, 5, 2).reshape(dec_batch, n_attn, N_HEADS, HEAD_W, past)
    cv = cache_v.reshape(dec_batch, n_attn, past, d)
    cos, sin_signed = _rope_tables(dec_seq)
    qk_tables = (jnp.stack([cos * Q_SCALE, cos]), jnp.stack([sin_signed * Q_SCALE, sin_signed]))
    cw8 = jnp.zeros((conv_w.shape[0], 8, d), F32).at[:, :conv_w.shape[1]].set(conv_w)
    g3 = ln_g.reshape(DEPTH, 1, d)
    b3 = ln_b.reshape(DEPTH, 1, d)
    subln = attn_subln_w.reshape(n_attn, 1, HEAD_W)
    conv_in, conv_out = conv_w_in.astype(BF16), conv_w_out.astype(BF16)
    attn_in, attn_out = attn_w_in.astype(BF16), attn_w_out.astype(BF16)

    attn_k_t = jnp.swapaxes(attn_in[:, :, d:2 * d], 1, 2)

    hp = hs = None
    kt_new = v_new = None
    for i in range(DEPTH):
        j = i // 2
        last = i == DEPTH - 1
        if i % 2 == 0:
            xp, hp = _conv_layer(hp, xp, mod, i, ctx_row, conv_in, cw8, conv_out, g3, b3, j, seq)
            xs, hs = _conv_layer(hs, xs, mod, i, lat_row, conv_in, cw8, conv_out, g3, b3, j, dec_seq)
        else:
            lam_init = 0.8 - 0.6 * math.exp(-0.3 * i)

            qp = _project(hp, attn_in, j, 0, d, BF16, scale=Q_SCALE)
            kt_new = _project_keys_t(hp, attn_k_t, j, kt_new, n_attn, batch, seq)
            v_new = _project(hp, attn_in, j, 2 * d, d, F32, cache=(v_new, n_attn, batch, seq))
            zp = _project(hp, attn_in, j, 3 * d, d, F32)
            yp = _attn_context(qp.reshape(batch, seq, d), kt_new, v_new, j, zp.reshape(batch, seq, d),
                               attn_lambda, subln, lam_init)
            outs = _attn_finish(yp.reshape(batch * seq, d), xp, mod, i, ctx_row, attn_out, g3, b3, j,
                                not last)
            xp, hp = outs if not last else (outs[0], None)

            qks = _project(hs, attn_in, j, 0, 2 * d, BF16, rope_tables=qk_tables, seq_len=dec_seq)
            vs = _project(hs, attn_in, j, 2 * d, d, BF16)
            zs = _project(hs, attn_in, j, 3 * d, d, F32)
            ys = _attn_latent(qks.reshape(dec_batch, dec_seq, 2 * d), vs.reshape(dec_batch, dec_seq, d),
                              ck, cv, j, zs.reshape(dec_batch, dec_seq, d), attn_lambda, subln, lam_init)
            outs = _attn_finish(ys.reshape(dec_batch * dec_seq, d), xs, mod, i, lat_row, attn_out,
                                g3, b3, j, not last)
            xs, hs = outs if not last else (outs[0], None)

    return (xp.reshape(batch, seq, d), xs.reshape(dec_batch, dec_seq, d),
            kt_new.reshape(batch, n_attn, N_HEADS, 2, QK_DIM, seq).transpose(0, 1, 5, 2, 3, 4),
            v_new.reshape(batch, n_attn, seq, N_HEADS, HEAD_W))
```
